```python
import jax, jax.numpy as jnp
from jax import lax
import numpy as np

D_MODEL = 1024
BATCH = 4
SEQ = 8192
DEPTH = 2
DEC_BATCH = 32
DEC_SEQ = 1
PAST_LEN = 16384
PAGE_SIZE = 128

N_A_LAYERS = DEPTH // 2
N_B_LAYERS = DEPTH - N_A_LAYERS
H_A = 4
D_INNER_A = D_MODEL
DK_A = D_INNER_A // H_A
DV_A = D_INNER_A // H_A
CONV_W = 4
CHUNK_A = 64
FORGET_BIAS = 3.0
GROUPS_B = ((128, 1), (512, 4), (2048, 16))
N_GROUPS_B = len(GROUPS_B)
H_G = 8
HD_B = 128
D_GROUP_B = H_G * HD_B
D_FF = 4 * D_MODEL
EPS = 1e-6

kernel_name = 'hybrid_mlstm_dilated_yoco_step'


def rmsnorm(x, g):
    xf = x.astype(jnp.float32)
    y = xf * lax.rsqrt(jnp.mean(xf * xf, axis=-1, keepdims=True) + EPS)
    return (y * g.astype(jnp.float32)).astype(x.dtype)


def ada_mod(c, w, b, n):
    m = (jax.nn.silu(c) @ w + b)[:, None, :]
    return jnp.split(m, n, axis=-1)


def to_heads(x, h):
    b_, t_, _ = x.shape
    return x.reshape(b_, t_, h, -1).transpose(0, 2, 1, 3)


def causal_conv(u, buf, w, b):
    t = u.shape[1]
    full = jnp.concatenate([buf.astype(u.dtype), u], axis=1)
    y = sum(full[:, j:j + t] * w[j] for j in range(CONV_W)) + b
    return y, full[:, t:]


def mlstm_scan(q, k, v, logi, logf, c0, n0, m0):
    bsz, h, t, _ = q.shape
    L = min(CHUNK_A, t)
    nc = -(-t // L)
    pad = nc * L - t

    def chunks(a, fill):
        a = jnp.pad(a, [(0, 0), (0, 0), (0, pad)] + [(0, 0)] * (a.ndim - 3), constant_values=fill)
        a = a.reshape(a.shape[:2] + (nc, L) + a.shape[3:])
        return jnp.moveaxis(a, 2, 0)

    xs = (chunks(q, 0.0), chunks(k, 0.0), chunks(v, 0.0), chunks(logi, -jnp.inf), chunks(logf, 0.0))
    causal = jnp.tril(jnp.ones((L, L), dtype=bool))

    def step(carry, inp):
        cm, nv, m = carry
        qc, kc, vc, li, lf = inp
        b = jnp.cumsum(lf, axis=-1)
        dmat = b[..., :, None] - b[..., None, :] + li[..., None, :]
        dmat = jnp.where(causal, dmat, -jnp.inf)
        inter = b + m[..., None]
        m_t = jnp.maximum(inter, jnp.max(dmat, axis=-1))
        w_inter = jnp.exp(inter - m_t)
        w_intra = jnp.exp(dmat - m_t[..., None])
        s = jnp.einsum('bhtk,bhsk->bhts', qc, kc) * w_intra
        num = w_inter[..., None] * jnp.einsum('bhtk,bhkv->bhtv', qc, cm) + jnp.einsum('bhts,bhsv->bhtv', s, vc)
        den = w_inter * jnp.einsum('bhtk,bhk->bht', qc, nv) + jnp.sum(s, axis=-1)
        hout = num / jnp.maximum(jnp.abs(den), jnp.exp(-m_t))[..., None]
        wl = w_intra[..., -1, :]
        c_new = w_inter[..., -1, None, None] * cm + jnp.einsum('bhs,bhsk,bhsv->bhkv', wl, kc, vc)
        n_new = w_inter[..., -1, None] * nv + jnp.einsum('bhs,bhsk->bhk', wl, kc)
        return (c_new, n_new, m_t[..., -1]), hout

    (cf, nf, mf), hs = lax.scan(step, (c0, n0, m0), xs)
    hs = jnp.moveaxis(hs, 0, 2).reshape(bsz, h, nc * L, -1)[:, :, :t]
    return hs, cf, nf, mf


def mlstm_mixer(h, conv_buf, c0, n0, m0, w_in, b_gate, w_conv, b_conv, w_out):
    bsz, t, _ = h.shape
    f32 = jnp.float32
    proj = h @ w_in
    qk_pre = proj[..., :2 * D_INNER_A]
    v = proj[..., 2 * D_INNER_A:3 * D_INNER_A]
    o_pre = proj[..., 3 * D_INNER_A:4 * D_INNER_A]
    gates = proj[..., 4 * D_INNER_A:].astype(f32) + b_gate.astype(f32)
    qk, conv_new = causal_conv(qk_pre, conv_buf, w_conv, b_conv)
    qk = jax.nn.silu(qk)
    q = to_heads(qk[..., :D_INNER_A], H_A).astype(f32)
    k = to_heads(qk[..., D_INNER_A:], H_A).astype(f32) * (DK_A ** -0.5)
    vh = to_heads(v, H_A).astype(f32)
    logi = gates[..., :H_A].transpose(0, 2, 1)
    logf = jax.nn.log_sigmoid(gates[..., H_A:]).transpose(0, 2, 1)
    hh, cf, nf, mf = mlstm_scan(q, k, vh, logi, logf, c0.astype(f32), n0.astype(f32), m0.astype(f32))
    hh = hh.transpose(0, 2, 1, 3).reshape(bsz, t, D_INNER_A).astype(h.dtype)
    out = (jax.nn.sigmoid(o_pre) * hh) @ w_out
    return out, cf, nf, mf, conv_new


def dilated_band_attention(q, k, v, window, dil):
    bsz, t, h, dh = q.shape
    M = window // dil
    ls = t // dil
    nb = -(-ls // M)
    pad = nb * M - ls

    def to_sub(a):
        a = a.reshape(bsz, ls, dil, h, dh).transpose(0, 2, 1, 3, 4).reshape(bsz * dil, ls, h, dh)
        a = jnp.pad(a, ((0, 0), (0, pad), (0, 0), (0, 0)))
        return a.reshape(bsz * dil, nb, M, h, dh)

    qs, ks, vs = to_sub(q), to_sub(k), to_sub(v)
    pw = ((0, 0), (1, 0), (0, 0), (0, 0), (0, 0))
    kk = jnp.concatenate([jnp.pad(ks, pw)[:, :-1], ks], axis=2)
    vv = jnp.concatenate([jnp.pad(vs, pw)[:, :-1], vs], axis=2)
    s = jnp.einsum('znqhd,znkhd->znhqk', qs, kk).astype(jnp.float32) * (dh ** -0.5)
    iq = jnp.arange(M)[:, None] + M
    ik = jnp.arange(2 * M)[None, :]
    rel = iq - ik
    band = (rel >= 0) & (rel <= M)
    first_ok = (jnp.arange(nb)[:, None] * M - M + jnp.arange(2 * M)[None, :]) >= 0
    valid = band[None] & first_ok[:, None, :]
    s = jnp.where(valid[None, :, None], s, -jnp.inf)
    lse = jax.nn.logsumexp(s, axis=-1)
    p = jnp.exp(s - lse[..., None])
    o = jnp.einsum('znhqk,znkhd->znqhd', p.astype(v.dtype), vv)
    o = o.reshape(bsz * dil, nb * M, h, dh)[:, :ls]
    o = o.reshape(bsz, dil, ls, h, dh).transpose(0, 2, 1, 3, 4).reshape(bsz, t, h, dh)
    lse = lse.transpose(0, 1, 3, 2).reshape(bsz * dil, nb * M, h)[:, :ls]
    lse = lse.reshape(bsz, dil, ls, h).transpose(0, 2, 1, 3).reshape(bsz, t, h)
    return o, lse


def dilated_gather_attention(q, k_new, v_new, k_cache, v_cache, window, dil):
    wb = k_cache.shape[1]
    s_len, dh = q.shape[1], q.shape[-1]
    M = window // dil
    kf = jnp.concatenate([k_cache.astype(k_new.dtype), k_new], axis=1)
    vf = jnp.concatenate([v_cache.astype(v_new.dtype), v_new], axis=1)
    idx = wb + jnp.arange(s_len)[:, None] - dil * jnp.arange(M + 1)[None, :]
    valid = idx >= 0
    idx = jnp.maximum(idx, 0)
    kg = kf[:, idx]
    vg = vf[:, idx]
    s = jnp.einsum('bshd,bsmhd->bshm', q, kg).astype(jnp.float32) * (dh ** -0.5)
    s = jnp.where(valid[None, :, None, :], s, -jnp.inf)
    lse = jax.nn.logsumexp(s, axis=-1)
    p = jnp.exp(s - lse[..., None])
    o = jnp.einsum('bshm,bsmhd->bshd', p.astype(vg.dtype), vg)
    return o, lse


def shared_kv(x, c, g_kv, w_ada_kv, b_ada_kv, w_kv):
    shift, scale = ada_mod(c, w_ada_kv, b_ada_kv, 2)
    hkv = rmsnorm(x, g_kv) * (1 + scale) + shift
    bsz, t, _ = x.shape
    kv = (hkv @ w_kv).reshape(bsz, t, 2, N_GROUPS_B, H_G, HD_B)
    return [(kv[:, :, 0, g], kv[:, :, 1, g]) for g in range(N_GROUPS_B)]


def dilated_mixer(h, kv, kv_caches, w_q, w_o):
    bsz, t, _ = h.shape
    q = (h @ w_q).reshape(bsz, t, N_GROUPS_B, H_G, HD_B)
    outs, lses = [], []
    for g, (win, dil) in enumerate(GROUPS_B):
        k_g, v_g = kv[g]
        if kv_caches is None:
            o, l = dilated_band_attention(q[:, :, g], k_g, v_g, win, dil)
        else:
            o, l = dilated_gather_attention(q[:, :, g], k_g, v_g, kv_caches[g][0], kv_caches[g][1], win, dil)
        outs.append(o)
        lses.append(l)
    wts = jax.nn.softmax(jnp.stack(lses, 0), axis=0)
    o = jnp.einsum('gbth,gbthd->bthd', wts, jnp.stack(outs, 0).astype(jnp.float32)).astype(h.dtype)
    return o.reshape(bsz, t, D_GROUP_B) @ w_o


def trunk(x, c, a_states, kv_caches, p):
    c_all, n_all, m_all, conv_all = a_states
    cs, ns, ms, convs = [], [], [], []
    kv = None
    for layer in range(DEPTH):
        sh1, sc1, gt1, sh2, sc2, gt2 = ada_mod(c, p['w_ada'][layer], p['b_ada'][layer], 6)
        gn = p['g_norm'][layer]
        h = rmsnorm(x, gn[0]) * (1 + sc1) + sh1
        if layer < N_A_LAYERS:
            a = layer
            h, cf, nf, mf, cb = mlstm_mixer(h, conv_all[a], c_all[a], n_all[a], m_all[a],
                                            p['w_a_in'][a], p['b_a_gate'][a], p['w_a_conv'][a],
                                            p['b_a_conv'][a], p['w_a_out'][a])
            cs.append(cf)
            ns.append(nf)
            ms.append(mf)
            convs.append(cb)
        else:
            if kv is None:
                kv = shared_kv(x, c, p['g_kv'], p['w_ada_kv'], p['b_ada_kv'], p['w_kv'])
            bl = layer - N_A_LAYERS
            h = dilated_mixer(h, kv, kv_caches, p['w_b_q'][bl], p['w_b_o'][bl])
        x = x + gt1 * rmsnorm(h, gn[1])
        h = rmsnorm(x, gn[2]) * (1 + sc2) + sh2
        h = jnp.square(jax.nn.relu(h @ p['w_mlp_up'][layer])) @ p['w_mlp_down'][layer]
        x = x + gt2 * rmsnorm(h, gn[3])
    dt = x.dtype
    states = (jnp.stack(cs).astype(dt), jnp.stack(ns).astype(dt), jnp.stack(ms).astype(dt), jnp.stack(convs).astype(dt))
    return x, states, kv


def setup_inputs(seed: int = 0) -> dict:
    key = jax.random.key(seed)
    ks = jax.random.split(key, 40)
    f32 = jnp.float32

    def nrm(k, shape, s):
        return jax.random.normal(k, shape, f32) * s

    d = D_MODEL
    inp = {}
    inp['x_prompt'] = nrm(ks[0], (BATCH, SEQ, d), 1.0)
    inp['x_sample'] = nrm(ks[1], (DEC_BATCH, DEC_SEQ, d), 1.0)
    inp['c_prompt'] = nrm(ks[2], (BATCH, d), 1.0)
    inp['c_sample'] = nrm(ks[3], (DEC_BATCH, d), 1.0)
    inp['state_C'] = nrm(ks[4], (N_A_LAYERS, DEC_BATCH, H_A, DK_A, DV_A), 0.05)
    inp['state_n'] = nrm(ks[5], (N_A_LAYERS, DEC_BATCH, H_A, DK_A), 0.05)
    inp['state_m'] = nrm(ks[6], (N_A_LAYERS, DEC_BATCH, H_A), 1.0)
    inp['state_conv'] = nrm(ks[7], (N_A_LAYERS, DEC_BATCH, CONV_W - 1, 2 * D_INNER_A), 1.0)
    for g, (win, dil) in enumerate(GROUPS_B):
        wb = min(win, PAST_LEN)
        inp['cache_k_g%d' % g] = nrm(ks[8 + 2 * g], (DEC_BATCH, wb, H_G, HD_B), 1.0)
        inp['cache_v_g%d' % g] = nrm(ks[9 + 2 * g], (DEC_BATCH, wb, H_G, HD_B), 1.0)
    inp['w_ada'] = nrm(ks[14], (DEPTH, d, 6 * d), 0.5 * d ** -0.5)
    inp['b_ada'] = nrm(ks[15], (DEPTH, 6 * d), 0.02)
    inp['g_norm'] = 1.0 + nrm(ks[16], (DEPTH, 4, d), 0.02)
    inp['w_mlp_up'] = nrm(ks[17], (DEPTH, d, D_FF), d ** -0.5)
    inp['w_mlp_down'] = nrm(ks[18], (DEPTH, D_FF, d), D_FF ** -0.5)
    inp['w_a_in'] = nrm(ks[19], (N_A_LAYERS, d, 4 * D_INNER_A + 2 * H_A), d ** -0.5)
    b_i = nrm(ks[20], (N_A_LAYERS, H_A), 0.1)
    b_f = FORGET_BIAS + nrm(ks[21], (N_A_LAYERS, H_A), 0.1)
    inp['b_a_gate'] = jnp.concatenate([b_i, b_f], axis=-1)
    inp['w_a_conv'] = nrm(ks[22], (N_A_LAYERS, CONV_W, 2 * D_INNER_A), CONV_W ** -0.5)
    inp['b_a_conv'] = nrm(ks[23], (N_A_LAYERS, 2 * D_INNER_A), 0.02)
    inp['w_a_out'] = nrm(ks[24], (N_A_LAYERS, D_INNER_A, d), D_INNER_A ** -0.5)
    inp['g_kv'] = 1.0 + nrm(ks[25], (d,), 0.02)
    inp['w_ada_kv'] = nrm(ks[26], (d, 2 * d), 0.5 * d ** -0.5)
    inp['b_ada_kv'] = nrm(ks[27], (2 * d,), 0.02)
    inp['w_kv'] = nrm(ks[28], (d, 2 * N_GROUPS_B * D_GROUP_B), d ** -0.5)
    inp['w_b_q'] = nrm(ks[29], (N_B_LAYERS, d, N_GROUPS_B * D_GROUP_B), d ** -0.5)
    inp['w_b_o'] = nrm(ks[30], (N_B_LAYERS, D_GROUP_B, d), D_GROUP_B ** -0.5)
    return inp


def reference(x_prompt, x_sample, c_prompt, c_sample, state_C, state_n, state_m, state_conv,
              cache_k_g0, cache_v_g0, cache_k_g1, cache_v_g1, cache_k_g2, cache_v_g2,
              w_ada, b_ada, g_norm, w_mlp_up, w_mlp_down, w_a_in, b_a_gate, w_a_conv, b_a_conv,
              w_a_out, g_kv, w_ada_kv, b_ada_kv, w_kv, w_b_q, w_b_o):
    p = {'w_ada': w_ada, 'b_ada': b_ada, 'g_norm': g_norm, 'w_mlp_up': w_mlp_up,
         'w_mlp_down': w_mlp_down, 'w_a_in': w_a_in, 'b_a_gate': b_a_gate, 'w_a_conv': w_a_conv,
         'b_a_conv': b_a_conv, 'w_a_out': w_a_out, 'g_kv': g_kv, 'w_ada_kv': w_ada_kv,
         'b_ada_kv': b_ada_kv, 'w_kv': w_kv, 'w_b_q': w_b_q, 'w_b_o': w_b_o}
    bp, t = x_prompt.shape[0], x_prompt.shape[1]
    f32 = jnp.float32
    init = (jnp.zeros((N_A_LAYERS, bp, H_A, DK_A, DV_A), f32),
            jnp.zeros((N_A_LAYERS, bp, H_A, DK_A), f32),
            jnp.zeros((N_A_LAYERS, bp, H_A), f32),
            jnp.zeros((N_A_LAYERS, bp, CONV_W - 1, 2 * D_INNER_A), x_prompt.dtype))
    y_prompt, (p_C, p_n, p_m, p_conv), kv_p = trunk(x_prompt, c_prompt, init, None, p)
    caches = [(cache_k_g0, cache_v_g0), (cache_k_g1, cache_v_g1), (cache_k_g2, cache_v_g2)]
    y_sample, (s_C, s_n, s_m, s_conv), kv_s = trunk(x_sample, c_sample, (state_C, state_n, state_m, state_conv), caches, p)
    r0 = min(GROUPS_B[0][0], t)
    r1 = min(GROUPS_B[1][0], t)
    r2 = min(GROUPS_B[2][0], t)
    return (y_prompt, y_sample, p_C, p_n, p_m, p_conv, s_C, s_n, s_m, s_conv,
            kv_p[0][0][:, -r0:], kv_p[0][1][:, -r0:], kv_p[1][0][:, -r1:], kv_p[1][1][:, -r1:],
            kv_p[2][0][:, -r2:], kv_p[2][1][:, -r2:],
            kv_s[0][0], kv_s[0][1], kv_s[1][0], kv_s[1][1], kv_s[2][0], kv_s[2][1])
```

```python
import functools

import jax
import jax.numpy as jnp
from jax import lax
from jax.experimental import pallas as pl
from jax.experimental.pallas import tpu as pltpu

F32 = jnp.float32
BF16 = jnp.bfloat16
EPS = 1e-6
NEG_INF = float("-inf")

H_A = 4
CONV_W = 4
GROUPS_B = ((128, 1), (512, 4), (2048, 16))
N_GROUPS = len(GROUPS_B)
H_G = 8
HD_B = 128
BAND = 128
LANES = 128
SUBLANES = 8
VMEM_LIMIT = 56 * 1024 * 1024


def _params(sem):
    return pltpu.CompilerParams(dimension_semantics=sem, vmem_limit_bytes=VMEM_LIMIT)


def _rms(x):
    return x * lax.rsqrt(jnp.mean(x * x, axis=-1, keepdims=True) + EPS)


def _sigmoid(x):
    return 1.0 / (1.0 + jnp.exp(-x))


def _log_sigmoid(x):
    return jnp.minimum(x, 0.0) - jnp.log1p(jnp.exp(-jnp.abs(x)))


def _resident(shape):
    nd = len(shape)
    return pl.BlockSpec(shape, lambda *_: (0,) * nd, pipeline_mode=pl.Buffered(1))


def _mod_spec(tm_mod, d, col):
    if tm_mod == 1:
        return pl.BlockSpec((1, 1, d), lambda b, i, *_: (b, 0, col))
    return pl.BlockSpec((1, tm_mod, d), lambda b, i, *_: (b, i, col))


def _ada_kernel(c_ref, w_ref, b_ref, o_ref):
    c = c_ref[...]
    s = (c * _sigmoid(c)).astype(BF16)
    o_ref[...] = jnp.dot(s, w_ref[...].astype(BF16), preferred_element_type=F32) + b_ref[...]


def _ada(c, w3, b3, layer):
    r, d = c.shape
    n = w3.shape[-1]
    tn = 1024
    return pl.pallas_call(
        _ada_kernel,
        grid=(n // tn,),
        in_specs=[pl.BlockSpec((r, d), lambda j: (0, 0)),
                  pl.BlockSpec((None, d, tn), lambda j: (layer, 0, j)),
                  pl.BlockSpec((None, 1, tn), lambda j: (layer, 0, j))],
        out_specs=pl.BlockSpec((r, tn), lambda j: (0, j)),
        out_shape=jax.ShapeDtypeStruct((r, n), F32),
        compiler_params=_params(("arbitrary",)),
        name="ada_mod",
    )(c, w3, b3)


def _proj_kernel(x_ref, g_ref, sc_ref, sh_ref, w_ref, *rest, heads, tail, out_scale):
    if tail:
        o_ref, t_ref, h_scr = rest
    else:
        o_ref, h_scr = rest
        t_ref = None
    j = pl.program_id(2)

    @pl.when(j == 0)
    def _():
        y = _rms(x_ref[0]) * g_ref[...]
        h_scr[...] = (y * (1.0 + sc_ref[0]) + sh_ref[0]).astype(BF16)

    acc = jnp.dot(h_scr[...], w_ref[j], preferred_element_type=F32)
    if heads:
        for hh in range(acc.shape[1] // HD_B):
            o_ref[0, hh] = (acc[:, hh * HD_B:(hh + 1) * HD_B] * out_scale).astype(o_ref.dtype)
    else:
        o_ref[0] = acc.astype(o_ref.dtype)
    if tail:
        i0 = tail
        @pl.when(pl.program_id(1) >= i0)
        def _():
            t_ref[0] = acc


def _proj(x, g, mod, sc_col, sh_col, w3, *, tm, heads=False, tail_rows=0, out_scale=1.0,
          out_dtype=F32):
    b, t, d = x.shape
    nj, _, tn = w3.shape
    tm = min(tm, t)
    ni = t // tm
    tm_mod = 1 if mod.shape[1] == 1 else tm
    in_specs = [pl.BlockSpec((1, tm, d), lambda b_, i, j: (b_, i, 0)),
                _resident((1, d)),
                _mod_spec(tm_mod, d, sc_col),
                _mod_spec(tm_mod, d, sh_col),
                _resident((nj, d, tn))]
    if heads:
        hpj = tn // HD_B
        out_shape = [jax.ShapeDtypeStruct((b, nj * hpj, t, HD_B), BF16)]
        out_specs = [pl.BlockSpec((1, hpj, tm, HD_B), lambda b_, i, j: (b_, j, i, 0))]
    else:
        out_shape = [jax.ShapeDtypeStruct((b, t, nj * tn), out_dtype)]
        out_specs = [pl.BlockSpec((1, tm, tn), lambda b_, i, j: (b_, i, j))]
    i0 = 0
    if tail_rows:
        i0 = ni - tail_rows // tm
        out_shape.append(jax.ShapeDtypeStruct((b, tail_rows, nj * tn), F32))
        out_specs.append(pl.BlockSpec(
            (1, tm, tn),
            lambda b_, i, j: (b_, jnp.maximum(i - i0, 0), jnp.where(i >= i0, j, 0))))
    kern = functools.partial(_proj_kernel, heads=heads, tail=i0 if tail_rows else 0,
                             out_scale=out_scale)
    res = pl.pallas_call(
        kern,
        grid=(b, ni, nj),
        in_specs=in_specs,
        out_specs=out_specs,
        out_shape=out_shape,
        scratch_shapes=[pltpu.VMEM((tm, d), BF16)],
        compiler_params=_params(("arbitrary", "arbitrary", "arbitrary")),
        name="proj",
    )(x, g, mod, mod, w3)
    return res if tail_rows else res[0]


def _inproj_kernel(x_ref, g_ref, sc_ref, sh_ref, w_ref, wg_ref, cw_ref, cb_ref, c0_ref,
                   qk_ref, v_ref, o_ref, gt_ref, cn_ref, ext_scr, *, tm, k_scale):
    i = pl.program_id(1)
    y = _rms(x_ref[0]) * g_ref[...]
    h = (y * (1.0 + sc_ref[0]) + sh_ref[0]).astype(BF16)

    @pl.when(i == 0)
    def _():
        ext_scr[0:SUBLANES, :] = c0_ref[0]

    di = w_ref.shape[2]
    for c in range(2):
        cs = slice(c * di, (c + 1) * di)
        ext_scr[SUBLANES:, cs] = jnp.dot(h, w_ref[c], preferred_element_type=F32)
        yv = cb_ref[:, cs]
        for k in range(CONV_W):
            yv = yv + cw_ref[CONV_W - 1 - k:CONV_W - k, cs] * ext_scr[SUBLANES - k:SUBLANES - k + tm, cs]
        yv = yv * _sigmoid(yv)
        if c == 1:
            yv = yv * k_scale
        qk_ref[0, :, cs] = yv.astype(BF16)
    v_ref[0] = jnp.dot(h, w_ref[2], preferred_element_type=F32).astype(BF16)
    o_ref[0] = jnp.dot(h, w_ref[3], preferred_element_type=F32)
    gt_ref[0] = jnp.dot(h, wg_ref[...], preferred_element_type=F32)
    last = ext_scr[tm:tm + SUBLANES, :]
    cn_ref[0] = last
    ext_scr[0:SUBLANES, :] = last


def _inproj_prompt(x, g, mod, w4, wg, cw, cb, conv0, *, tm):
    b, t, d = x.shape
    di = w4.shape[2]
    ni = t // tm
    dk = di // H_A
    kern = functools.partial(_inproj_kernel, tm=tm, k_scale=dk ** -0.5)
    row = lambda n: pl.BlockSpec((1, tm, n), lambda b_, i: (b_, i, 0))
    return pl.pallas_call(
        kern,
        grid=(b, ni),
        in_specs=[row(d), _resident((1, d)), _mod_spec(1, d, 1), _mod_spec(1, d, 0),
                  _resident(w4.shape), _resident(wg.shape), _resident(cw.shape), _resident(cb.shape),
                  pl.BlockSpec((1, SUBLANES, 2 * di), lambda b_, i: (b_, 0, 0))],
        out_specs=[row(2 * di), row(di), row(di), row(LANES),
                   pl.BlockSpec((1, SUBLANES, 2 * di), lambda b_, i: (b_, 0, 0))],
        out_shape=[jax.ShapeDtypeStruct((b, t, 2 * di), BF16),
                   jax.ShapeDtypeStruct((b, t, di), BF16),
                   jax.ShapeDtypeStruct((b, t, di), F32),
                   jax.ShapeDtypeStruct((b, t, LANES), F32),
                   jax.ShapeDtypeStruct((b, SUBLANES, 2 * di), F32)],
        scratch_shapes=[pltpu.VMEM((tm + SUBLANES, 2 * di), F32)],
        compiler_params=_params(("arbitrary", "arbitrary")),
        name="inproj_prompt",
    )(x, g, mod, mod, w4, wg, cw, cb, conv0)


def _mlstm_kernel(qk_ref, v_ref, o_ref, gt_ref, bg_ref, c0_ref, n0_ref, m0_ref,
                  hg_ref, c_ref, n_ref, m_ref, *, L, dk):
    i = pl.program_id(1)

    @pl.when(i == 0)
    def _():
        c_ref[...] = c0_ref[...]
        n_ref[...] = n0_ref[...]
        m_ref[...] = m0_ref[...]

    di = H_A * dk
    g = gt_ref[0] + bg_ref[...]
    lane = lax.broadcasted_iota(jnp.int32, (L, LANES), 1)
    gl = jnp.where(lane < H_A, g, _log_sigmoid(g))
    gl_t = gl.T
    r8 = gl_t[0:SUBLANES, :]
    pos = lax.broadcasted_iota(jnp.int32, (SUBLANES, L), 1)
    cs = r8
    s = 1
    while s < L:
        cs = cs + jnp.where(pos >= s, pltpu.roll(cs, s, axis=1), 0.0)
        s *= 2
    cs_t = jnp.concatenate([cs, jnp.zeros((LANES - SUBLANES, L), F32)], axis=0).T
    row = lax.broadcasted_iota(jnp.int32, (L, L), 0)
    col = lax.broadcasted_iota(jnp.int32, (L, L), 1)
    causal = col <= row
    m_all = m_ref[0]
    lane1 = lax.broadcasted_iota(jnp.int32, (1, LANES), 1)
    for h in range(H_A):
        q = qk_ref[0, :, h * dk:(h + 1) * dk]
        k = qk_ref[0, :, di + h * dk:di + (h + 1) * dk]
        v = v_ref[0, :, h * dk:(h + 1) * dk]
        b_col = cs_t[:, H_A + h:H_A + h + 1]
        b_row = cs[H_A + h:H_A + h + 1, :]
        li_row = r8[h:h + 1, :]
        li_col = gl[:, h:h + 1]
        m_prev = m_all[:, h:h + 1]
        dmat = jnp.where(causal, b_col - b_row + li_row, NEG_INF)
        inter = b_col + m_prev
        m_t = jnp.maximum(inter, jnp.max(dmat, axis=1, keepdims=True))
        w_inter = jnp.exp(inter - m_t)
        w_intra = jnp.exp(dmat - m_t)
        s_mat = lax.dot_general(q, k, (((1,), (1,)), ((), ())), preferred_element_type=F32) * w_intra
        cm = c_ref[0, h]
        nv = n_ref[0, h:h + 1, :]
        num = (w_inter * jnp.dot(q, cm.astype(BF16), preferred_element_type=F32)
               + jnp.dot(s_mat.astype(BF16), v, preferred_element_type=F32))
        qn = jnp.sum(q.astype(F32) * nv, axis=1, keepdims=True)
        den = w_inter * qn + jnp.sum(s_mat, axis=1, keepdims=True)
        hout = num / jnp.maximum(jnp.abs(den), jnp.exp(-m_t))
        hg_ref[0, :, h * dk:(h + 1) * dk] = (_sigmoid(o_ref[0, :, h * dk:(h + 1) * dk]) * hout).astype(BF16)
        m_last = m_t[L - 1:L, :]
        b_last = b_col[L - 1:L, :]
        wl_col = jnp.exp(b_last - b_col + li_col - m_last)
        w_last = w_inter[L - 1:L, :]
        kw = k.astype(F32) * wl_col
        c_ref[0, h] = w_last * cm + lax.dot_general(
            kw.astype(BF16), v, (((0,), (0,)), ((), ())), preferred_element_type=F32)
        n_ref[0, h:h + 1, :] = w_last * nv + jnp.sum(kw, axis=0, keepdims=True)
        m_all = jnp.where(lane1 == h, m_last, m_all)
    m_ref[0] = m_all


def _mlstm_prompt(qk, v, o_pre, gates, bg, c0, n0, m0, *, L):
    b, t, di = v.shape
    dk = di // H_A
    nc = t // L
    kern = functools.partial(_mlstm_kernel, L=L, dk=dk)
    row = lambda n: pl.BlockSpec((1, L, n), lambda b_, i: (b_, i, 0))
    st4 = pl.BlockSpec((1, H_A, dk, dk), lambda b_, i: (b_, 0, 0, 0))
    st3 = pl.BlockSpec((1, H_A, dk), lambda b_, i: (b_, 0, 0))
    stm = pl.BlockSpec((1, 1, LANES), lambda b_, i: (b_, 0, 0))
    return pl.pallas_call(
        kern,
        grid=(b, nc),
        in_specs=[row(2 * di), row(di), row(di), row(LANES), _resident((1, LANES)), st4, st3, stm],
        out_specs=[row(di), st4, st3, stm],
        out_shape=[jax.ShapeDtypeStruct((b, t, di), BF16),
                   jax.ShapeDtypeStruct((b, H_A, dk, dk), F32),
                   jax.ShapeDtypeStruct((b, H_A, dk), F32),
                   jax.ShapeDtypeStruct((b, 1, LANES), F32)],
        compiler_params=_params(("arbitrary", "arbitrary")),
        name="mlstm_prompt",
    )(qk, v, o_pre, gates, bg, c0, n0, m0)


def _mlstm_step_kernel(pj_ref, gt_ref, bg_ref, cw_ref, cb_ref, cv_ref, c0_ref, n0_ref, m0_ref,
                       hg_ref, c_ref, n_ref, m_ref, cvn_ref, *, dk):
    di = H_A * dk
    pj = pj_ref[0]
    u = pj[:, 0:2 * di]
    buf = cv_ref[0]
    yv = cb_ref[...] + cw_ref[CONV_W - 1:CONV_W, :] * u
    for j in range(CONV_W - 1):
        yv = yv + cw_ref[j:j + 1, :] * buf[j:j + 1, :]
    yv = yv * _sigmoid(yv)
    cvn_ref[0] = jnp.concatenate([buf[1:CONV_W - 1, :], u], axis=0)
    g = gt_ref[0] + bg_ref[...]
    m_all = m0_ref[0]
    lane1 = lax.broadcasted_iota(jnp.int32, (1, LANES), 1)
    row8 = lax.broadcasted_iota(jnp.int32, (SUBLANES, dk), 0)
    for h in range(H_A):
        q = yv[:, h * dk:(h + 1) * dk]
        k = yv[:, di + h * dk:di + (h + 1) * dk] * (dk ** -0.5)
        v = pj[:, 2 * di + h * dk:2 * di + (h + 1) * dk]
        o_pre = pj[:, 3 * di + h * dk:3 * di + (h + 1) * dk]
        li = g[:, h:h + 1]
        lf = _log_sigmoid(g[:, H_A + h:H_A + h + 1])
        m_prev = m_all[:, h:h + 1]
        inter = lf + m_prev
        m_t = jnp.maximum(inter, li)
        w_inter = jnp.exp(inter - m_t)
        w_intra = jnp.exp(li - m_t)
        cm = c0_ref[0, h]
        nv = n0_ref[0, h:h + 1, :]
        q8 = jnp.where(row8 == 0, jnp.broadcast_to(q, (SUBLANES, dk)), 0.0).astype(BF16)
        s = jnp.sum(q * k, axis=1, keepdims=True) * w_intra
        qc = jnp.dot(q8, cm.astype(BF16), preferred_element_type=F32)[0:1, :]
        num = w_inter * qc + s * v
        den = w_inter * jnp.sum(q * nv, axis=1, keepdims=True) + s
        hout = num / jnp.maximum(jnp.abs(den), jnp.exp(-m_t))
        hg_ref[0, :, h * dk:(h + 1) * dk] = _sigmoid(o_pre) * hout
        kw = k * w_intra
        kw8 = jnp.where(row8 == 0, jnp.broadcast_to(kw, (SUBLANES, dk)), 0.0).astype(BF16)
        v8 = jnp.where(row8 == 0, jnp.broadcast_to(v, (SUBLANES, dk)), 0.0).astype(BF16)
        c_ref[0, h] = w_inter * cm + lax.dot_general(
            kw8, v8, (((0,), (0,)), ((), ())), preferred_element_type=F32)
        n_ref[0, h:h + 1, :] = w_inter * nv + kw
        m_all = jnp.where(lane1 == h, m_t, m_all)
    m_ref[0] = m_all


def _mlstm_step(pj, gates, bg, cw, cb, conv, c0, n0, m0):
    s_, _, n4 = pj.shape
    di = n4 // 4
    dk = di // H_A
    kern = functools.partial(_mlstm_step_kernel, dk=dk)
    per = lambda *shape: pl.BlockSpec((1,) + shape, lambda b_: (b_,) + (0,) * len(shape))
    return pl.pallas_call(
        kern,
        grid=(s_,),
        in_specs=[per(1, n4), per(1, LANES), _resident((1, LANES)), _resident(cw.shape),
                  _resident(cb.shape), per(CONV_W - 1, 2 * di), per(H_A, dk, dk), per(H_A, dk),
                  per(1, LANES)],
        out_specs=[per(1, di), per(H_A, dk, dk), per(H_A, dk), per(1, LANES), per(CONV_W - 1, 2 * di)],
        out_shape=[jax.ShapeDtypeStruct((s_, 1, di), F32),
                   jax.ShapeDtypeStruct((s_, H_A, dk, dk), F32),
                   jax.ShapeDtypeStruct((s_, H_A, dk), F32),
                   jax.ShapeDtypeStruct((s_, 1, LANES), F32),
                   jax.ShapeDtypeStruct((s_, CONV_W - 1, 2 * di), F32)],
        compiler_params=_params(("arbitrary",)),
        name="mlstm_step",
    )(pj, gates, bg, cw, cb, conv, c0, n0, m0)


def _post_kernel(*refs, attn, ff_chunk):
    if attn:
        og = refs[0:N_GROUPS]
        ls = refs[N_GROUPS:2 * N_GROUPS]
        rest = refs[2 * N_GROUPS:]
        l = [r[0] for r in ls]
        mx = functools.reduce(jnp.maximum, l)
        e = [jnp.exp(a - mx) for a in l]
        inv = 1.0 / functools.reduce(lambda a, b_: a + b_, e)
        w = [a * inv for a in e]
        parts = []
        for h in range(H_G):
            acc_h = w[0][:, h:h + 1] * og[0][0, h].astype(F32)
            for gi in range(1, N_GROUPS):
                acc_h = acc_h + w[gi][:, h:h + 1] * og[gi][0, h].astype(F32)
            parts.append(acc_h.astype(BF16))
        lhs = jnp.concatenate(parts, axis=1)
    else:
        lhs = refs[0][0].astype(BF16)
        rest = refs[1:]
    x_ref, gt1_ref, sh2_ref, sc2_ref, gt2_ref, gn_ref, wo_ref, wup_ref, wdn_ref, out_ref = rest
    a = jnp.dot(lhs, wo_ref[...], preferred_element_type=F32)
    x1 = x_ref[0] + gt1_ref[0] * (_rms(a) * gn_ref[1:2, :])
    h2 = (_rms(x1) * gn_ref[2:3, :] * (1.0 + sc2_ref[0]) + sh2_ref[0]).astype(BF16)
    dff = wup_ref.shape[1]
    acc = None
    for c in range(dff // ff_chunk):
        cs = slice(c * ff_chunk, (c + 1) * ff_chunk)
        u = jnp.maximum(jnp.dot(h2, wup_ref[:, cs], preferred_element_type=F32), 0.0)
        p = jnp.dot((u * u).astype(BF16), wdn_ref[cs, :], preferred_element_type=F32)
        acc = p if acc is None else acc + p
    out_ref[0] = x1 + gt2_ref[0] * (_rms(acc) * gn_ref[3:4, :])


def _post(lhs, x, mod, gn, wo, wup, wdn, *, tm, lses=None):
    b, t, d = x.shape
    tm = min(tm, t)
    ni = t // tm
    tm_mod = 1 if mod.shape[1] == 1 else tm
    attn = lses is not None
    row = lambda n: pl.BlockSpec((1, tm, n), lambda b_, i: (b_, i, 0))
    if attn:
        lhs_specs = ([pl.BlockSpec((1, H_G, tm, HD_B), lambda b_, i: (b_, 0, i, 0))] * N_GROUPS
                     + [row(H_G)] * N_GROUPS)
        lhs_args = list(lhs) + list(lses)
    else:
        lhs_specs = [row(lhs.shape[-1])]
        lhs_args = [lhs]
    kern = functools.partial(_post_kernel, attn=attn, ff_chunk=1024)
    return pl.pallas_call(
        kern,
        grid=(b, ni),
        in_specs=lhs_specs + [row(d), _mod_spec(tm_mod, d, 2), _mod_spec(tm_mod, d, 3),
                              _mod_spec(tm_mod, d, 4), _mod_spec(tm_mod, d, 5),
                              _resident(gn.shape), _resident(wo.shape), _resident(wup.shape),
                              _resident(wdn.shape)],
        out_specs=row(d),
        out_shape=jax.ShapeDtypeStruct((b, t, d), F32),
        compiler_params=_params(("arbitrary", "arbitrary")),
        name="post_mlp",
    )(*lhs_args, x, mod, mod, mod, mod, gn, wo, wup, wdn)


def _band_kernel(q_ref, k_ref, v_ref, kh_ref, vh_ref, o_ref, l_ref, kb_scr, vb_scr, l_scr,
                 *, dil, nb, hps):
    n = pl.program_id(2)
    l_scr[...] = jnp.zeros(l_scr.shape, F32)
    kb_scr[:, 0:BAND, :] = kh_ref[0]
    kb_scr[:, BAND:, :] = k_ref[0]
    vb_scr[:, 0:BAND, :] = vh_ref[0]
    vb_scr[:, BAND:, :] = v_ref[0]
    iq = lax.broadcasted_iota(jnp.int32, (BAND, 2 * BAND), 0)
    ik = lax.broadcasted_iota(jnp.int32, (BAND, 2 * BAND), 1)
    band = (ik >= iq) & (ik <= iq + BAND)
    lane = lax.broadcasted_iota(jnp.int32, (BAND, LANES), 1)
    for r in range(dil):
        ls = slice(r * HD_B, (r + 1) * HD_B)

        def body(it, carry, r=r, ls=ls):
            h = it // nb
            jb = it % nb
            r0 = pl.multiple_of(jb * BAND, BAND)
            q = q_ref[0, h, pl.ds(r0, BAND), ls]
            kw = kb_scr[h, pl.ds(r0, 2 * BAND), ls]
            vw = vb_scr[h, pl.ds(r0, 2 * BAND), ls]
            s = lax.dot_general(q, kw, (((1,), (1,)), ((), ())), preferred_element_type=F32)
            kmin = jnp.where((n * nb + jb) == 0, BAND, 0)
            ok = band & (ik >= kmin)
            s = jnp.where(ok, s, NEG_INF)
            m = jnp.max(s, axis=1, keepdims=True)
            p = jnp.exp(s - m)
            den = jnp.sum(p, axis=1, keepdims=True)
            o = jnp.dot(p.astype(BF16), vw, preferred_element_type=F32) / den
            o_ref[0, h, pl.ds(r0, BAND), ls] = o.astype(o_ref.dtype)
            lse = m + jnp.log(den)
            cur = l_scr[pl.ds(r0, BAND), :]
            l_scr[pl.ds(r0, BAND), :] = jnp.where(lane == r * hps + h, lse, cur)
            return carry

        lax.fori_loop(0, hps * nb, body, 0)
    l_ref[0, 0] = l_scr[:, 0:dil * hps]


def _band_attention(q, kv, grp, *, tokens_per_step=2048, hps=4):
    b, _, t, _ = q.shape
    dil = GROUPS_B[grp][1]
    w = dil * HD_B
    tsub = t // dil
    rows = tokens_per_step // dil
    nb = rows // BAND
    nsteps = tsub // rows
    nhb = H_G // hps
    qv = q.reshape(b, N_GROUPS * H_G, tsub, w)
    kvv = kv.reshape(b, 2 * N_GROUPS * H_G, tsub, w)
    cur = lambda off: pl.BlockSpec((1, hps, rows, w), lambda b_, hb, n: (b_, off * nhb + hb, n, 0))
    halo = lambda off: pl.BlockSpec(
        (1, hps, BAND, w), lambda b_, hb, n: (b_, off * nhb + hb, jnp.maximum(n * nb - 1, 0), 0))
    kern = functools.partial(_band_kernel, dil=dil, nb=nb, hps=hps)
    o, l = pl.pallas_call(
        kern,
        grid=(b, nhb, nsteps),
        in_specs=[cur(grp), cur(grp), cur(N_GROUPS + grp), halo(grp), halo(N_GROUPS + grp)],
        out_specs=[pl.BlockSpec((1, hps, rows, w), lambda b_, hb, n: (b_, hb, n, 0)),
                   pl.BlockSpec((1, 1, rows, dil * hps), lambda b_, hb, n: (b_, hb, n, 0))],
        out_shape=[jax.ShapeDtypeStruct((b, H_G, tsub, w), BF16),
                   jax.ShapeDtypeStruct((b, nhb, tsub, dil * hps), F32)],
        scratch_shapes=[pltpu.VMEM((hps, rows + BAND, w), BF16),
                        pltpu.VMEM((hps, rows + BAND, w), BF16),
                        pltpu.VMEM((rows, LANES), F32)],
        compiler_params=_params(("arbitrary", "arbitrary", "arbitrary")),
        name="band_attn_g%d" % grp,
    )(qv, kvv, kvv, kvv, kvv)
    o = o.reshape(b, H_G, t, HD_B)
    l = l.reshape(b, nhb, tsub, dil, hps).transpose(0, 2, 3, 1, 4).reshape(b, t, H_G)
    return o, l


def _gather_attn_kernel(q_ref, kv_ref, *rest, scale):
    caches = rest[0:2 * N_GROUPS]
    o_ref = rest[2 * N_GROUPS]
    q = q_ref[0]
    kvn = kv_ref[0]
    dg = H_G * HD_B
    outs, lses = [], []
    for gi in range(N_GROUPS):
        kc = caches[2 * gi][0]
        vc = caches[2 * gi + 1][0]
        og, lg = [], []
        for h in range(H_G):
            hs = slice(h * HD_B, (h + 1) * HD_B)
            qh = q[:, gi * dg + h * HD_B:gi * dg + (h + 1) * HD_B]
            kn = kvn[:, gi * dg + h * HD_B:gi * dg + (h + 1) * HD_B]
            vn = kvn[:, (N_GROUPS + gi) * dg + h * HD_B:(N_GROUPS + gi) * dg + (h + 1) * HD_B]
            sc = jnp.sum(kc[:, hs] * qh, axis=1, keepdims=True) * scale
            sn = jnp.sum(kn * qh, axis=1, keepdims=True) * scale
            m = jnp.maximum(jnp.max(sc, axis=0, keepdims=True), sn)
            pc = jnp.exp(sc - m)
            pn = jnp.exp(sn - m)
            den = jnp.sum(pc, axis=0, keepdims=True) + pn
            og.append((jnp.sum(pc * vc[:, hs], axis=0, keepdims=True) + pn * vn) / den)
            lg.append(m + jnp.log(den))
        outs.append(og)
        lses.append(lg)
    for h in range(H_G):
        l = [lses[gi][h] for gi in range(N_GROUPS)]
        mx = functools.reduce(jnp.maximum, l)
        e = [jnp.exp(a - mx) for a in l]
        inv = 1.0 / functools.reduce(lambda a, b_: a + b_, e)
        acc = (e[0] * inv) * outs[0][h]
        for gi in range(1, N_GROUPS):
            acc = acc + (e[gi] * inv) * outs[gi][h]
        o_ref[0, :, h * HD_B:(h + 1) * HD_B] = acc


def _gather_attention(q, kvn, caches):
    s_ = q.shape[0]
    dg = H_G * HD_B
    args, specs = [], []
    for gi, (win, dil) in enumerate(GROUPS_B):
        for c in caches[gi]:
            wb = c.shape[1]
            assert wb == win and wb // dil == BAND
            args.append(c.reshape(s_, BAND, dil * dg))
            specs.append(pl.BlockSpec((1, BAND, dg), lambda b_: (b_, 0, 0)))
    per = lambda n: pl.BlockSpec((1, 1, n), lambda b_: (b_, 0, 0))
    kern = functools.partial(_gather_attn_kernel, scale=HD_B ** -0.5)
    return pl.pallas_call(
        kern,
        grid=(s_,),
        in_specs=[per(N_GROUPS * dg), per(2 * N_GROUPS * dg)] + specs,
        out_specs=per(dg),
        out_shape=jax.ShapeDtypeStruct((s_, 1, dg), F32),
        compiler_params=_params(("arbitrary",)),
        name="gather_attn",
    )(q, kvn, *args)


def kernel(x_prompt, x_sample, c_prompt, c_sample, state_C, state_n, state_m, state_conv,
           cache_k_g0, cache_v_g0, cache_k_g1, cache_v_g1, cache_k_g2, cache_v_g2,
           w_ada, b_ada, g_norm, w_mlp_up, w_mlp_down, w_a_in, b_a_gate, w_a_conv, b_a_conv,
           w_a_out, g_kv, w_ada_kv, b_ada_kv, w_kv, w_b_q, w_b_o):
    bp, t, d = x_prompt.shape
    s_ = x_sample.shape[0]
    assert x_sample.shape[1] == 1 and w_ada.shape[0] == 2
    di = w_a_out.shape[1]
    dk = di // H_A
    dg = H_G * HD_B
    tn = 1024

    def chunks(w, n):
        return w.astype(BF16).reshape(w.shape[0], n // tn, tn).transpose(1, 0, 2)

    w_in = w_a_in[0]
    w_in4 = chunks(w_in[:, :4 * di], 4 * di)
    w_gate = jnp.pad(w_in[:, 4 * di:], ((0, 0), (0, LANES - 2 * H_A))).astype(BF16)
    b_gate = jnp.pad(b_a_gate[0], (0, LANES - 2 * H_A)).reshape(1, LANES)
    w_q3 = chunks(w_b_q[0], N_GROUPS * dg)
    w_kv6 = chunks(w_kv, 2 * N_GROUPS * dg)
    w_out = w_a_out[0].astype(BF16)
    w_o = w_b_o[0].astype(BF16)
    w_up = w_mlp_up.astype(BF16)
    w_dn = w_mlp_down.astype(BF16)
    cw = w_a_conv[0]
    cb = b_a_conv[0].reshape(1, 2 * di)
    g_kv2 = g_kv.reshape(1, d)

    c_all = jnp.concatenate([c_sample, c_prompt], axis=0)
    b_ada3 = b_ada.reshape(2, 1, 6 * d)
    mods = [_ada(c_all, w_ada, b_ada3, layer) for layer in range(2)]
    mod_kv = _ada(c_all, w_ada_kv.reshape(1, d, 2 * d), b_ada_kv.reshape(1, 1, 2 * d), 0)
    mp = [m[s_:].reshape(bp, 1, -1) for m in mods + [mod_kv]]
    ms = [m[:s_].reshape(1, s_, -1) for m in mods + [mod_kv]]

    conv0 = jnp.zeros((bp, SUBLANES, 2 * di), F32)
    qk, v, o_pre, gates, conv_new = _inproj_prompt(
        x_prompt, g_norm[0, 0:1], mp[0], w_in4, w_gate, cw, cb, conv0, tm=512)
    hg, p_c, p_n, p_m = _mlstm_prompt(
        qk, v, o_pre, gates, b_gate,
        jnp.zeros((bp, H_A, dk, dk), F32), jnp.zeros((bp, H_A, dk), F32),
        jnp.zeros((bp, 1, LANES), F32), L=256)
    x1 = _post(hg, x_prompt, mp[0], g_norm[0], w_out, w_up[0], w_dn[0], tm=512)
    max_win = max(wn for wn, _ in GROUPS_B)
    kv_h, kv_tail = _proj(x1, g_kv2, mp[2], 1, 0, w_kv6, tm=1024, heads=True, tail_rows=max_win)
    q_h = _proj(x1, g_norm[1, 0:1], mp[1], 1, 0, w_q3, tm=1024, heads=True,
                out_scale=HD_B ** -0.5)
    outs, lses = [], []
    for gi in range(N_GROUPS):
        o_g, l_g = _band_attention(q_h, kv_h, gi)
        outs.append(o_g)
        lses.append(l_g)
    y_prompt = _post(outs, x1, mp[1], g_norm[1], w_o, w_up[1], w_dn[1], tm=512, lses=lses)
    p_kv = []
    for gi, (win, _) in enumerate(GROUPS_B):
        for half in range(2):
            col = (half * N_GROUPS + gi) * dg
            p_kv.append(kv_tail[:, max_win - win:, col:col + dg].reshape(bp, win, H_G, HD_B))

    xs = x_sample.reshape(1, s_, d)
    pj = _proj(xs, g_norm[0, 0:1], ms[0], 1, 0, w_in4, tm=s_)
    gts = _proj(xs, g_norm[0, 0:1], ms[0], 1, 0, w_gate.reshape(1, d, LANES), tm=s_)
    m0 = jnp.pad(state_m[0], ((0, 0), (0, LANES - H_A))).reshape(s_, 1, LANES)
    hgs, s_c, s_n, s_m, s_conv = _mlstm_step(
        pj.reshape(s_, 1, 4 * di), gts.reshape(s_, 1, LANES), b_gate, cw, cb,
        state_conv[0], state_C[0], state_n[0], m0)
    xs1 = _post(hgs.reshape(1, s_, di), xs, ms[0], g_norm[0], w_out, w_up[0], w_dn[0], tm=s_)
    kv_s = _proj(xs1, g_kv2, ms[2], 1, 0, w_kv6, tm=s_)
    q_s = _proj(xs1, g_norm[1, 0:1], ms[1], 1, 0, w_q3, tm=s_)
    caches = [(cache_k_g0, cache_v_g0), (cache_k_g1, cache_v_g1), (cache_k_g2, cache_v_g2)]
    att = _gather_attention(q_s.reshape(s_, 1, N_GROUPS * dg), kv_s.reshape(s_, 1, 2 * N_GROUPS * dg),
                            caches)
    y_sample = _post(att.reshape(1, s_, dg), xs1, ms[1], g_norm[1], w_o, w_up[1], w_dn[1], tm=s_)
    s_kv = []
    for gi in range(N_GROUPS):
        for half in range(2):
            col = (half * N_GROUPS + gi) * dg
            s_kv.append(kv_s[0, :, col:col + dg].reshape(s_, 1, H_G, HD_B))

    return (y_prompt, y_sample.reshape(s_, 1, d),
            p_c[None], p_n[None], p_m[:, 0, :H_A][None], conv_new[:, SUBLANES - (CONV_W - 1):][None],
            s_c[None], s_n[None], s_m[:, 0, :H_A][None], s_conv[None],
            *p_kv, *s_kv)
```

```python
import functools

import jax
import jax.numpy as jnp
from jax import lax
from jax.experimental import pallas as pl
from jax.experimental.pallas import tpu as pltpu

F32 = jnp.float32
BF16 = jnp.bfloat16
EPS = 1e-6
NEG_INF = float("-inf")

H_A = 4
CONV_W = 4
GROUPS_B = ((128, 1), (512, 4), (2048, 16))
N_GROUPS = len(GROUPS_B)
DILS = tuple(dil for _, dil in GROUPS_B)
H_G = 8
HD_B = 128
BAND = 128
LANES = 128
SUBLANES = 8
VMEM_LIMIT = 56 * 1024 * 1024


def _params(sem):
    return pltpu.CompilerParams(dimension_semantics=sem, vmem_limit_bytes=VMEM_LIMIT)


def _rms(x):
    return x * lax.rsqrt(jnp.mean(x * x, axis=-1, keepdims=True) + EPS)


def _sigmoid(x):
    return 1.0 / (1.0 + jnp.exp(-x))


def _log_sigmoid(x):
    return jnp.minimum(x, 0.0) - jnp.log1p(jnp.exp(-jnp.abs(x)))


def _resident(shape):
    nd = len(shape)
    return pl.BlockSpec(shape, lambda *_: (0,) * nd, pipeline_mode=pl.Buffered(1))


def _mod_spec(tm_mod, d, col):
    if tm_mod == 1:
        return pl.BlockSpec((1, 1, d), lambda b, i, *_: (b, 0, col))
    return pl.BlockSpec((1, tm_mod, d), lambda b, i, *_: (b, i, col))


def _ada_kernel(c_ref, w_ref, b_ref, o_ref):
    c = c_ref[...]
    s = (c * _sigmoid(c)).astype(BF16)
    o_ref[...] = jnp.dot(s, w_ref[...].astype(BF16), preferred_element_type=F32) + b_ref[...]


def _ada(c, w3, b3, layer):
    r, d = c.shape
    n = w3.shape[-1]
    tn = 1024
    return pl.pallas_call(
        _ada_kernel,
        grid=(n // tn,),
        in_specs=[pl.BlockSpec((r, d), lambda j: (0, 0)),
                  pl.BlockSpec((None, d, tn), lambda j: (layer, 0, j)),
                  pl.BlockSpec((None, 1, tn), lambda j: (layer, 0, j))],
        out_specs=pl.BlockSpec((r, tn), lambda j: (0, j)),
        out_shape=jax.ShapeDtypeStruct((r, n), F32),
        compiler_params=_params(("arbitrary",)),
        name="ada_mod",
    )(c, w3, b3)


def _proj_kernel(x_ref, g_ref, sc_ref, sh_ref, w_ref, o_ref, h_scr):
    j = pl.program_id(2)

    @pl.when(j == 0)
    def _():
        y = _rms(x_ref[0]) * g_ref[...]
        h_scr[...] = (y * (1.0 + sc_ref[0]) + sh_ref[0]).astype(BF16)

    o_ref[0] = jnp.dot(h_scr[...], w_ref[j], preferred_element_type=F32)


def _proj(x, g, mod, sc_col, sh_col, w3, *, tm):
    b, t, d = x.shape
    nj, _, tn = w3.shape
    tm = min(tm, t)
    ni = t // tm
    tm_mod = 1 if mod.shape[1] == 1 else tm
    return pl.pallas_call(
        _proj_kernel,
        grid=(b, ni, nj),
        in_specs=[pl.BlockSpec((1, tm, d), lambda b_, i, j: (b_, i, 0)),
                  _resident((1, d)),
                  _mod_spec(tm_mod, d, sc_col),
                  _mod_spec(tm_mod, d, sh_col),
                  _resident((nj, d, tn))],
        out_specs=pl.BlockSpec((1, tm, tn), lambda b_, i, j: (b_, i, j)),
        out_shape=jax.ShapeDtypeStruct((b, t, nj * tn), F32),
        scratch_shapes=[pltpu.VMEM((tm, d), BF16)],
        compiler_params=_params(("arbitrary", "arbitrary", "arbitrary")),
        name="proj",
    )(x, g, mod, mod, w3)


def _proj_dil_kernel(x_ref, g_ref, sc_ref, sh_ref, w_ref, *rest, nj, tail_i0, out_scale, tm):
    outs = rest[:nj]
    if tail_i0 is None:
        t_ref = None
        h_scr, slab = rest[nj:]
    else:
        t_ref = rest[nj]
        h_scr, slab = rest[nj + 1:]
    i = pl.program_id(1)
    j = pl.program_id(2)

    @pl.when(j == 0)
    def _():
        y = _rms(x_ref[0]) * g_ref[...]
        h = y * (1.0 + sc_ref[0]) + sh_ref[0]
        nslab = h.shape[1] // LANES
        for c in range(nslab):
            slab[c] = h[:, c * LANES:(c + 1) * LANES]
        for gi, dil in enumerate(DILS):
            if dil == 1:
                h_scr[gi] = h.astype(BF16)
                continue
            rows = tm // dil
            for r in range(dil):
                for c in range(nslab):
                    h_scr[gi, r * rows:(r + 1) * rows, c * LANES:(c + 1) * LANES] = (
                        slab[c, pl.ds(r, rows, stride=dil), :].astype(BF16))

    acc = jnp.dot(h_scr[j % N_GROUPS], w_ref[j], preferred_element_type=F32)
    for jj in range(nj):
        dil = DILS[jj % N_GROUPS]
        rows = tm // dil

        @pl.when(j == jj)
        def _(jj=jj, dil=dil, rows=rows):
            for hh in range(H_G):
                for r in range(dil):
                    blk = acc[r * rows:(r + 1) * rows, hh * HD_B:(hh + 1) * HD_B]
                    if out_scale != 1.0:
                        blk = blk * out_scale
                    outs[jj][0, hh, :, r * HD_B:(r + 1) * HD_B] = blk.astype(BF16)
            if tail_i0 is not None:
                @pl.when(i >= tail_i0)
                def _():
                    if dil == 1:
                        t_ref[0] = acc
                    else:
                        t_ref[0] = jnp.dot(h_scr[0], w_ref[jj], preferred_element_type=F32)


def _proj_dil(x, g, mod, sc_col, sh_col, w3, *, tm, tail_rows=0, out_scale=1.0):
    b, t, d = x.shape
    nj, _, tn = w3.shape
    assert tn == H_G * HD_B and tm % (16 * max(DILS)) == 0 and t % tm == 0
    ni = t // tm
    in_specs = [pl.BlockSpec((1, tm, d), lambda b_, i, j: (b_, i, 0)),
                _resident((1, d)), _mod_spec(1, d, sc_col), _mod_spec(1, d, sh_col),
                _resident((nj, d, tn))]
    out_shape, out_specs = [], []
    for jj in range(nj):
        dil = DILS[jj % N_GROUPS]
        out_shape.append(jax.ShapeDtypeStruct((b, H_G, t // dil, dil * HD_B), BF16))
        out_specs.append(pl.BlockSpec((1, H_G, tm // dil, dil * HD_B), lambda b_, i, j: (b_, 0, i, 0)))
    i0 = None
    if tail_rows:
        i0 = ni - tail_rows // tm
        out_shape.append(jax.ShapeDtypeStruct((b, tail_rows, nj * tn), F32))
        out_specs.append(pl.BlockSpec(
            (1, tm, tn),
            lambda b_, i, j: (b_, jnp.maximum(i - i0, 0), jnp.where(i >= i0, j, 0))))
    kern = functools.partial(_proj_dil_kernel, nj=nj, tail_i0=i0, out_scale=out_scale, tm=tm)
    return pl.pallas_call(
        kern,
        grid=(b, ni, nj),
        in_specs=in_specs,
        out_specs=out_specs,
        out_shape=out_shape,
        scratch_shapes=[pltpu.VMEM((N_GROUPS, tm, d), BF16),
                        pltpu.VMEM((d // LANES, tm, LANES), F32)],
        compiler_params=_params(("arbitrary", "arbitrary", "arbitrary")),
        name="proj_dil",
    )(x, g, mod, mod, w3)


def _inproj_kernel(x_ref, g_ref, sc_ref, sh_ref, w_ref, wg_ref, cw_ref, cb_ref, c0_ref,
                   qk_ref, v_ref, o_ref, gt_ref, cn_ref, ext_scr, *, tm, k_scale):
    i = pl.program_id(1)
    y = _rms(x_ref[0]) * g_ref[...]
    h = (y * (1.0 + sc_ref[0]) + sh_ref[0]).astype(BF16)

    @pl.when(i == 0)
    def _():
        ext_scr[0:SUBLANES, :] = c0_ref[0]

    di = w_ref.shape[2]
    for c in range(2):
        cs = slice(c * di, (c + 1) * di)
        ext_scr[SUBLANES:, cs] = jnp.dot(h, w_ref[c], preferred_element_type=F32)
        yv = cb_ref[:, cs]
        for k in range(CONV_W):
            yv = yv + cw_ref[CONV_W - 1 - k:CONV_W - k, cs] * ext_scr[SUBLANES - k:SUBLANES - k + tm, cs]
        yv = yv * _sigmoid(yv)
        if c == 1:
            yv = yv * k_scale
        qk_ref[0, :, cs] = yv.astype(BF16)
    v_ref[0] = jnp.dot(h, w_ref[2], preferred_element_type=F32).astype(BF16)
    o_ref[0] = jnp.dot(h, w_ref[3], preferred_element_type=F32)
    gt_ref[0] = jnp.dot(h, wg_ref[...], preferred_element_type=F32)
    last = ext_scr[tm:tm + SUBLANES, :]
    cn_ref[0] = last
    ext_scr[0:SUBLANES, :] = last


def _inproj_prompt(x, g, mod, w4, wg, cw, cb, conv0, *, tm):
    b, t, d = x.shape
    di = w4.shape[2]
    ni = t // tm
    dk = di // H_A
    kern = functools.partial(_inproj_kernel, tm=tm, k_scale=dk ** -0.5)
    row = lambda n: pl.BlockSpec((1, tm, n), lambda b_, i: (b_, i, 0))
    return pl.pallas_call(
        kern,
        grid=(b, ni),
        in_specs=[row(d), _resident((1, d)), _mod_spec(1, d, 1), _mod_spec(1, d, 0),
                  _resident(w4.shape), _resident(wg.shape), _resident(cw.shape), _resident(cb.shape),
                  pl.BlockSpec((1, SUBLANES, 2 * di), lambda b_, i: (b_, 0, 0))],
        out_specs=[row(2 * di), row(di), row(di), row(LANES),
                   pl.BlockSpec((1, SUBLANES, 2 * di), lambda b_, i: (b_, 0, 0))],
        out_shape=[jax.ShapeDtypeStruct((b, t, 2 * di), BF16),
                   jax.ShapeDtypeStruct((b, t, di), BF16),
                   jax.ShapeDtypeStruct((b, t, di), F32),
                   jax.ShapeDtypeStruct((b, t, LANES), F32),
                   jax.ShapeDtypeStruct((b, SUBLANES, 2 * di), F32)],
        scratch_shapes=[pltpu.VMEM((tm + SUBLANES, 2 * di), F32)],
        compiler_params=_params(("arbitrary", "arbitrary")),
        name="inproj_prompt",
    )(x, g, mod, mod, w4, wg, cw, cb, conv0)


def _mlstm_kernel(qk_ref, v_ref, o_ref, gt_ref, bg_ref, c0_ref, n0_ref, m0_ref,
                  hg_ref, c_ref, n_ref, m_ref, *, L, dk):
    i = pl.program_id(1)

    @pl.when(i == 0)
    def _():
        c_ref[...] = c0_ref[...]
        n_ref[...] = n0_ref[...]
        m_ref[...] = m0_ref[...]

    di = H_A * dk
    g = gt_ref[0] + bg_ref[...]
    lane = lax.broadcasted_iota(jnp.int32, (L, LANES), 1)
    gl = jnp.where(lane < H_A, g, _log_sigmoid(g))
    gl_t = gl.T
    r8 = gl_t[0:SUBLANES, :]
    pos = lax.broadcasted_iota(jnp.int32, (SUBLANES, L), 1)
    cs = r8
    s = 1
    while s < L:
        cs = cs + jnp.where(pos >= s, pltpu.roll(cs, s, axis=1), 0.0)
        s *= 2
    cs_t = jnp.concatenate([cs, jnp.zeros((LANES - SUBLANES, L), F32)], axis=0).T
    row = lax.broadcasted_iota(jnp.int32, (L, L), 0)
    col = lax.broadcasted_iota(jnp.int32, (L, L), 1)
    causal = col <= row
    m_all = m_ref[0]
    lane1 = lax.broadcasted_iota(jnp.int32, (1, LANES), 1)
    for h in range(H_A):
        q = qk_ref[0, :, h * dk:(h + 1) * dk]
        k = qk_ref[0, :, di + h * dk:di + (h + 1) * dk]
        v = v_ref[0, :, h * dk:(h + 1) * dk]
        b_col = cs_t[:, H_A + h:H_A + h + 1]
        b_row = cs[H_A + h:H_A + h + 1, :]
        li_row = r8[h:h + 1, :]
        li_col = gl[:, h:h + 1]
        m_prev = m_all[:, h:h + 1]
        dmat = jnp.where(causal, b_col - b_row + li_row, NEG_INF)
        inter = b_col + m_prev
        m_t = jnp.maximum(inter, jnp.max(dmat, axis=1, keepdims=True))
        w_inter = jnp.exp(inter - m_t)
        w_intra = jnp.exp(dmat - m_t)
        s_mat = lax.dot_general(q, k, (((1,), (1,)), ((), ())), preferred_element_type=F32) * w_intra
        cm = c_ref[0, h]
        nv = n_ref[0, h:h + 1, :]
        num = (w_inter * jnp.dot(q, cm.astype(BF16), preferred_element_type=F32)
               + jnp.dot(s_mat.astype(BF16), v, preferred_element_type=F32))
        qn = jnp.sum(q.astype(F32) * nv, axis=1, keepdims=True)
        den = w_inter * qn + jnp.sum(s_mat, axis=1, keepdims=True)
        hout = num / jnp.maximum(jnp.abs(den), jnp.exp(-m_t))
        hg_ref[0, :, h * dk:(h + 1) * dk] = (_sigmoid(o_ref[0, :, h * dk:(h + 1) * dk]) * hout).astype(BF16)
        m_last = m_t[L - 1:L, :]
        b_last = b_col[L - 1:L, :]
        wl_col = jnp.exp(b_last - b_col + li_col - m_last)
        w_last = w_inter[L - 1:L, :]
        kw = k.astype(F32) * wl_col
        c_ref[0, h] = w_last * cm + lax.dot_general(
            kw.astype(BF16), v, (((0,), (0,)), ((), ())), preferred_element_type=F32)
        n_ref[0, h:h + 1, :] = w_last * nv + jnp.sum(kw, axis=0, keepdims=True)
        m_all = jnp.where(lane1 == h, m_last, m_all)
    m_ref[0] = m_all


def _mlstm_prompt(qk, v, o_pre, gates, bg, c0, n0, m0, *, L):
    b, t, di = v.shape
    dk = di // H_A
    nc = t // L
    kern = functools.partial(_mlstm_kernel, L=L, dk=dk)
    row = lambda n: pl.BlockSpec((1, L, n), lambda b_, i: (b_, i, 0))
    st4 = pl.BlockSpec((1, H_A, dk, dk), lambda b_, i: (b_, 0, 0, 0))
    st3 = pl.BlockSpec((1, H_A, dk), lambda b_, i: (b_, 0, 0))
    stm = pl.BlockSpec((1, 1, LANES), lambda b_, i: (b_, 0, 0))
    return pl.pallas_call(
        kern,
        grid=(b, nc),
        in_specs=[row(2 * di), row(di), row(di), row(LANES), _resident((1, LANES)), st4, st3, stm],
        out_specs=[row(di), st4, st3, stm],
        out_shape=[jax.ShapeDtypeStruct((b, t, di), BF16),
                   jax.ShapeDtypeStruct((b, H_A, dk, dk), F32),
                   jax.ShapeDtypeStruct((b, H_A, dk), F32),
                   jax.ShapeDtypeStruct((b, 1, LANES), F32)],
        compiler_params=_params(("arbitrary", "arbitrary")),
        name="mlstm_prompt",
    )(qk, v, o_pre, gates, bg, c0, n0, m0)


def _mlstm_step_kernel(pj_ref, gt_ref, bg_ref, cw_ref, cb_ref, cv_ref, c0_ref, n0_ref, m0_ref,
                       hg_ref, c_ref, n_ref, m_ref, cvn_ref, *, dk):
    di = H_A * dk
    pj = pj_ref[0]
    u = pj[:, 0:2 * di]
    buf = cv_ref[0]
    yv = cb_ref[...] + cw_ref[CONV_W - 1:CONV_W, :] * u
    for j in range(CONV_W - 1):
        yv = yv + cw_ref[j:j + 1, :] * buf[j:j + 1, :]
    yv = yv * _sigmoid(yv)
    cvn_ref[0] = jnp.concatenate([buf[1:CONV_W - 1, :], u], axis=0)
    g = gt_ref[0] + bg_ref[...]
    m_all = m0_ref[0]
    lane1 = lax.broadcasted_iota(jnp.int32, (1, LANES), 1)
    row8 = lax.broadcasted_iota(jnp.int32, (SUBLANES, dk), 0)
    for h in range(H_A):
        q = yv[:, h * dk:(h + 1) * dk]
        k = yv[:, di + h * dk:di + (h + 1) * dk] * (dk ** -0.5)
        v = pj[:, 2 * di + h * dk:2 * di + (h + 1) * dk]
        o_pre = pj[:, 3 * di + h * dk:3 * di + (h + 1) * dk]
        li = g[:, h:h + 1]
        lf = _log_sigmoid(g[:, H_A + h:H_A + h + 1])
        m_prev = m_all[:, h:h + 1]
        inter = lf + m_prev
        m_t = jnp.maximum(inter, li)
        w_inter = jnp.exp(inter - m_t)
        w_intra = jnp.exp(li - m_t)
        cm = c0_ref[0, h]
        nv = n0_ref[0, h:h + 1, :]
        q8 = jnp.where(row8 == 0, jnp.broadcast_to(q, (SUBLANES, dk)), 0.0).astype(BF16)
        s = jnp.sum(q * k, axis=1, keepdims=True) * w_intra
        qc = jnp.dot(q8, cm.astype(BF16), preferred_element_type=F32)[0:1, :]
        num = w_inter * qc + s * v
        den = w_inter * jnp.sum(q * nv, axis=1, keepdims=True) + s
        hout = num / jnp.maximum(jnp.abs(den), jnp.exp(-m_t))
        hg_ref[0, :, h * dk:(h + 1) * dk] = _sigmoid(o_pre) * hout
        kw = k * w_intra
        kw8 = jnp.where(row8 == 0, jnp.broadcast_to(kw, (SUBLANES, dk)), 0.0).astype(BF16)
        v8 = jnp.where(row8 == 0, jnp.broadcast_to(v, (SUBLANES, dk)), 0.0).astype(BF16)
        c_ref[0, h] = w_inter * cm + lax.dot_general(
            kw8, v8, (((0,), (0,)), ((), ())), preferred_element_type=F32)
        n_ref[0, h:h + 1, :] = w_inter * nv + kw
        m_all = jnp.where(lane1 == h, m_t, m_all)
    m_ref[0] = m_all


def _mlstm_step(pj, gates, bg, cw, cb, conv, c0, n0, m0):
    s_, _, n4 = pj.shape
    di = n4 // 4
    dk = di // H_A
    kern = functools.partial(_mlstm_step_kernel, dk=dk)
    per = lambda *shape: pl.BlockSpec((1,) + shape, lambda b_: (b_,) + (0,) * len(shape))
    return pl.pallas_call(
        kern,
        grid=(s_,),
        in_specs=[per(1, n4), per(1, LANES), _resident((1, LANES)), _resident(cw.shape),
                  _resident(cb.shape), per(CONV_W - 1, 2 * di), per(H_A, dk, dk), per(H_A, dk),
                  per(1, LANES)],
        out_specs=[per(1, di), per(H_A, dk, dk), per(H_A, dk), per(1, LANES), per(CONV_W - 1, 2 * di)],
        out_shape=[jax.ShapeDtypeStruct((s_, 1, di), F32),
                   jax.ShapeDtypeStruct((s_, H_A, dk, dk), F32),
                   jax.ShapeDtypeStruct((s_, H_A, dk), F32),
                   jax.ShapeDtypeStruct((s_, 1, LANES), F32),
                   jax.ShapeDtypeStruct((s_, CONV_W - 1, 2 * di), F32)],
        compiler_params=_params(("arbitrary",)),
        name="mlstm_step",
    )(pj, gates, bg, cw, cb, conv, c0, n0, m0)


def _post_kernel(*refs, attn, ff_chunk, tm):
    if attn:
        og = refs[0:N_GROUPS]
        ls = refs[N_GROUPS:2 * N_GROUPS]
        rest = refs[2 * N_GROUPS:-1]
        nat = refs[-1]
        for gi, dil in enumerate(DILS):
            rows = tm // dil
            for h in range(H_G):
                for r in range(dil):
                    blk = og[gi][0, h, :, r * HD_B:(r + 1) * HD_B].astype(F32)
                    if dil == 1:
                        nat[gi, h] = blk
                    else:
                        nat[gi, h, pl.ds(r, rows, stride=dil), :] = blk
        l = [r[0] for r in ls]
        mx = functools.reduce(jnp.maximum, l)
        e = [jnp.exp(a - mx) for a in l]
        inv = 1.0 / functools.reduce(lambda a, b_: a + b_, e)
        w = [a * inv for a in e]
        parts = []
        for h in range(H_G):
            acc_h = w[0][:, h:h + 1] * nat[0, h]
            for gi in range(1, N_GROUPS):
                acc_h = acc_h + w[gi][:, h:h + 1] * nat[gi, h]
            parts.append(acc_h.astype(BF16))
        lhs = jnp.concatenate(parts, axis=1)
    else:
        lhs = refs[0][0].astype(BF16)
        rest = refs[1:]
    x_ref, gt1_ref, sh2_ref, sc2_ref, gt2_ref, gn_ref, wo_ref, wup_ref, wdn_ref, out_ref = rest
    a = jnp.dot(lhs, wo_ref[...], preferred_element_type=F32)
    x1 = x_ref[0] + gt1_ref[0] * (_rms(a) * gn_ref[1:2, :])
    h2 = (_rms(x1) * gn_ref[2:3, :] * (1.0 + sc2_ref[0]) + sh2_ref[0]).astype(BF16)
    dff = wup_ref.shape[1]
    acc = None
    for c in range(dff // ff_chunk):
        cs = slice(c * ff_chunk, (c + 1) * ff_chunk)
        u = jnp.maximum(jnp.dot(h2, wup_ref[:, cs], preferred_element_type=F32), 0.0)
        p = jnp.dot((u * u).astype(BF16), wdn_ref[cs, :], preferred_element_type=F32)
        acc = p if acc is None else acc + p
    out_ref[0] = x1 + gt2_ref[0] * (_rms(acc) * gn_ref[3:4, :])


def _post(lhs, x, mod, gn, wo, wup, wdn, *, tm, lses=None):
    b, t, d = x.shape
    tm = min(tm, t)
    ni = t // tm
    tm_mod = 1 if mod.shape[1] == 1 else tm
    attn = lses is not None
    row = lambda n: pl.BlockSpec((1, tm, n), lambda b_, i: (b_, i, 0))
    scratch = []
    if attn:
        lhs_specs = ([pl.BlockSpec((1, H_G, tm // dil, dil * HD_B), lambda b_, i: (b_, 0, i, 0))
                      for dil in DILS] + [row(H_G)] * N_GROUPS)
        lhs_args = list(lhs) + list(lses)
        scratch = [pltpu.VMEM((N_GROUPS, H_G, tm, HD_B), F32)]
    else:
        lhs_specs = [row(lhs.shape[-1])]
        lhs_args = [lhs]
    kern = functools.partial(_post_kernel, attn=attn, ff_chunk=1024, tm=tm)
    return pl.pallas_call(
        kern,
        grid=(b, ni),
        in_specs=lhs_specs + [row(d), _mod_spec(tm_mod, d, 2), _mod_spec(tm_mod, d, 3),
                              _mod_spec(tm_mod, d, 4), _mod_spec(tm_mod, d, 5),
                              _resident(gn.shape), _resident(wo.shape), _resident(wup.shape),
                              _resident(wdn.shape)],
        out_specs=row(d),
        out_shape=jax.ShapeDtypeStruct((b, t, d), F32),
        scratch_shapes=scratch,
        compiler_params=_params(("arbitrary", "arbitrary")),
        name="post_mlp",
    )(*lhs_args, x, mod, mod, mod, mod, gn, wo, wup, wdn)


def _band_kernel(q_ref, k_ref, v_ref, kh_ref, vh_ref, o_ref, l_ref, *, dil, nb, hps):
    n = pl.program_id(2)
    iq = lax.broadcasted_iota(jnp.int32, (BAND, 2 * BAND), 0)
    ik = lax.broadcasted_iota(jnp.int32, (BAND, 2 * BAND), 1)
    band = (ik >= iq) & (ik <= iq + BAND)
    lane = lax.broadcasted_iota(jnp.int32, (BAND, LANES), 1)

    def block(q, kw, vw, ok, r):
        s = jnp.einsum("hqd,hkd->hqk", q, kw, preferred_element_type=F32)
        s = jnp.where(ok[None], s, NEG_INF)
        m = jnp.max(s, axis=-1, keepdims=True)
        p = jnp.exp(s - m)
        den = jnp.sum(p, axis=-1, keepdims=True)
        o = jnp.einsum("hqk,hkd->hqd", p.astype(BF16), vw, preferred_element_type=F32) / den
        lse = m + jnp.log(den)
        tile = jnp.zeros((BAND, LANES), F32)
        for h in range(hps):
            tile = jnp.where(lane == r * hps + h, lse[h], tile)
        return o, tile

    for r in range(dil):
        ls = slice(r * HD_B, (r + 1) * HD_B)
        lsl = slice(r * hps, (r + 1) * hps)
        kw0 = jnp.concatenate([kh_ref[0, :, :, ls], k_ref[0, :, 0:BAND, ls]], axis=1)
        vw0 = jnp.concatenate([vh_ref[0, :, :, ls], v_ref[0, :, 0:BAND, ls]], axis=1)
        kmin = jnp.where(n == 0, BAND, 0)
        o, tile = block(q_ref[0, :, 0:BAND, ls], kw0, vw0, band & (ik >= kmin), r)
        o_ref[0, :, 0:BAND, ls] = o.astype(o_ref.dtype)
        l_ref[0, 0, 0:BAND, lsl] = tile[:, lsl]

        def body(jb, carry, r=r, ls=ls, lsl=lsl):
            r0 = pl.multiple_of(jb * BAND, BAND)
            rk = pl.multiple_of((jb - 1) * BAND, BAND)
            o, tile = block(q_ref[0, :, pl.ds(r0, BAND), ls], k_ref[0, :, pl.ds(rk, 2 * BAND), ls],
                            v_ref[0, :, pl.ds(rk, 2 * BAND), ls], band, r)
            o_ref[0, :, pl.ds(r0, BAND), ls] = o.astype(o_ref.dtype)
            l_ref[0, 0, pl.ds(r0, BAND), lsl] = tile[:, lsl]
            return carry

        if nb > 1:
            lax.fori_loop(1, nb, body, 0)


def _band_attention(q, k, v, grp, *, tokens_per_step=2048, hps=4):
    b, _, tsub, w = q.shape
    dil = DILS[grp]
    t = tsub * dil
    rows = tokens_per_step // dil
    nb = rows // BAND
    nsteps = tsub // rows
    nhb = H_G // hps
    cur = pl.BlockSpec((1, hps, rows, w), lambda b_, hb, n: (b_, hb, n, 0))
    halo = pl.BlockSpec((1, hps, BAND, w), lambda b_, hb, n: (b_, hb, jnp.maximum(n * nb - 1, 0), 0))
    kern = functools.partial(_band_kernel, dil=dil, nb=nb, hps=hps)
    o, l = pl.pallas_call(
        kern,
        grid=(b, nhb, nsteps),
        in_specs=[cur, cur, cur, halo, halo],
        out_specs=[cur, pl.BlockSpec((1, 1, rows, dil * hps), lambda b_, hb, n: (b_, hb, n, 0))],
        out_shape=[jax.ShapeDtypeStruct((b, H_G, tsub, w), BF16),
                   jax.ShapeDtypeStruct((b, nhb, tsub, dil * hps), F32)],
        compiler_params=_params(("arbitrary", "arbitrary", "arbitrary")),
        name="band_attn_g%d" % grp,
    )(q, k, v, k, v)
    l = l.reshape(b, nhb, tsub, dil, hps).transpose(0, 2, 3, 1, 4).reshape(b, t, H_G)
    return o, l


def _gather_attn_kernel(q_ref, kv_ref, *rest, scale):
    caches = rest[0:2 * N_GROUPS]
    o_ref = rest[2 * N_GROUPS]
    outs, lses = [], []
    for gi in range(N_GROUPS):
        kc = caches[2 * gi][0]
        vc = caches[2 * gi + 1][0]
        qg = q_ref[0, gi * H_G:(gi + 1) * H_G, :]
        kn = kv_ref[0, gi * H_G:(gi + 1) * H_G, :]
        vn = kv_ref[0, (N_GROUPS + gi) * H_G:(N_GROUPS + gi + 1) * H_G, :]
        sc = jnp.sum(kc * qg[None], axis=-1, keepdims=True) * scale
        sn = jnp.sum(kn * qg, axis=-1, keepdims=True) * scale
        m = jnp.maximum(jnp.max(sc, axis=0), sn)
        pc = jnp.exp(sc - m[None])
        pn = jnp.exp(sn - m)
        den = jnp.sum(pc, axis=0) + pn
        outs.append((jnp.sum(pc * vc, axis=0) + pn * vn) / den)
        lses.append(m + jnp.log(den))
    mx = functools.reduce(jnp.maximum, lses)
    e = [jnp.exp(a - mx) for a in lses]
    inv = 1.0 / functools.reduce(lambda a, b_: a + b_, e)
    acc = (e[0] * inv) * outs[0]
    for gi in range(1, N_GROUPS):
        acc = acc + (e[gi] * inv) * outs[gi]
    o_ref[0] = acc


def _gather_attention(q, kvn, caches):
    s_ = q.shape[0]
    args, specs = [], []
    for gi, (win, dil) in enumerate(GROUPS_B):
        for c in caches[gi]:
            wb = c.shape[1]
            assert wb == win and wb // dil == BAND
            args.append(c.reshape(s_, BAND, dil, H_G, HD_B))
            specs.append(pl.BlockSpec((1, BAND, None, H_G, HD_B), lambda b_: (b_, 0, 0, 0, 0)))
    per = lambda n: pl.BlockSpec((1, n, HD_B), lambda b_: (b_, 0, 0))
    kern = functools.partial(_gather_attn_kernel, scale=HD_B ** -0.5)
    return pl.pallas_call(
        kern,
        grid=(s_,),
        in_specs=[per(N_GROUPS * H_G), per(2 * N_GROUPS * H_G)] + specs,
        out_specs=per(H_G),
        out_shape=jax.ShapeDtypeStruct((s_, H_G, HD_B), F32),
        compiler_params=_params(("arbitrary",)),
        name="gather_attn",
    )(q, kvn, *args)


def kernel(x_prompt, x_sample, c_prompt, c_sample, state_C, state_n, state_m, state_conv,
           cache_k_g0, cache_v_g0, cache_k_g1, cache_v_g1, cache_k_g2, cache_v_g2,
           w_ada, b_ada, g_norm, w_mlp_up, w_mlp_down, w_a_in, b_a_gate, w_a_conv, b_a_conv,
           w_a_out, g_kv, w_ada_kv, b_ada_kv, w_kv, w_b_q, w_b_o):
    bp, t, d = x_prompt.shape
    s_ = x_sample.shape[0]
    assert x_sample.shape[1] == 1 and w_ada.shape[0] == 2
    di = w_a_out.shape[1]
    dk = di // H_A
    dg = H_G * HD_B
    tn = 1024

    def chunks(w, n):
        return w.astype(BF16).reshape(w.shape[0], n // tn, tn).transpose(1, 0, 2)

    w_in = w_a_in[0]
    w_in4 = chunks(w_in[:, :4 * di], 4 * di)
    w_gate = jnp.pad(w_in[:, 4 * di:], ((0, 0), (0, LANES - 2 * H_A))).astype(BF16)
    b_gate = jnp.pad(b_a_gate[0], (0, LANES - 2 * H_A)).reshape(1, LANES)
    w_q3 = chunks(w_b_q[0], N_GROUPS * dg)
    w_kv6 = chunks(w_kv, 2 * N_GROUPS * dg)
    w_out = w_a_out[0].astype(BF16)
    w_o = w_b_o[0].astype(BF16)
    w_up = w_mlp_up.astype(BF16)
    w_dn = w_mlp_down.astype(BF16)
    cw = w_a_conv[0]
    cb = b_a_conv[0].reshape(1, 2 * di)
    g_kv2 = g_kv.reshape(1, d)

    c_all = jnp.concatenate([c_sample, c_prompt], axis=0)
    b_ada3 = b_ada.reshape(2, 1, 6 * d)
    mods = [_ada(c_all, w_ada, b_ada3, layer) for layer in range(2)]
    mod_kv = _ada(c_all, w_ada_kv.reshape(1, d, 2 * d), b_ada_kv.reshape(1, 1, 2 * d), 0)
    mp = [m[s_:].reshape(bp, 1, -1) for m in mods + [mod_kv]]
    ms = [m[:s_].reshape(1, s_, -1) for m in mods + [mod_kv]]

    conv0 = jnp.zeros((bp, SUBLANES, 2 * di), F32)
    qk, v, o_pre, gates, conv_new = _inproj_prompt(
        x_prompt, g_norm[0, 0:1], mp[0], w_in4, w_gate, cw, cb, conv0, tm=512)
    hg, p_c, p_n, p_m = _mlstm_prompt(
        qk, v, o_pre, gates, b_gate,
        jnp.zeros((bp, H_A, dk, dk), F32), jnp.zeros((bp, H_A, dk), F32),
        jnp.zeros((bp, 1, LANES), F32), L=256)
    x1 = _post(hg, x_prompt, mp[0], g_norm[0], w_out, w_up[0], w_dn[0], tm=512)
    max_win = max(wn for wn, _ in GROUPS_B)
    *kv_d, kv_tail = _proj_dil(x1, g_kv2, mp[2], 1, 0, w_kv6, tm=512, tail_rows=max_win)
    q_d = _proj_dil(x1, g_norm[1, 0:1], mp[1], 1, 0, w_q3, tm=512, out_scale=HD_B ** -0.5)
    outs, lses = [], []
    for gi in range(N_GROUPS):
        o_g, l_g = _band_attention(q_d[gi], kv_d[gi], kv_d[N_GROUPS + gi], gi)
        outs.append(o_g)
        lses.append(l_g)
    y_prompt = _post(outs, x1, mp[1], g_norm[1], w_o, w_up[1], w_dn[1], tm=512, lses=lses)
    p_kv = []
    for gi, (win, _) in enumerate(GROUPS_B):
        for half in range(2):
            col = (half * N_GROUPS + gi) * dg
            p_kv.append(kv_tail[:, max_win - win:, col:col + dg].reshape(bp, win, H_G, HD_B))

    xs = x_sample.reshape(1, s_, d)
    pj = _proj(xs, g_norm[0, 0:1], ms[0], 1, 0, w_in4, tm=s_)
    gts = _proj(xs, g_norm[0, 0:1], ms[0], 1, 0, w_gate.reshape(1, d, LANES), tm=s_)
    m0 = jnp.pad(state_m[0], ((0, 0), (0, LANES - H_A))).reshape(s_, 1, LANES)
    hgs, s_c, s_n, s_m, s_conv = _mlstm_step(
        pj.reshape(s_, 1, 4 * di), gts.reshape(s_, 1, LANES), b_gate, cw, cb,
        state_conv[0], state_C[0], state_n[0], m0)
    xs1 = _post(hgs.reshape(1, s_, di), xs, ms[0], g_norm[0], w_out, w_up[0], w_dn[0], tm=s_)
    kv_s = _proj(xs1, g_kv2, ms[2], 1, 0, w_kv6, tm=s_)
    q_s = _proj(xs1, g_norm[1, 0:1], ms[1], 1, 0, w_q3, tm=s_)
    caches = [(cache_k_g0, cache_v_g0), (cache_k_g1, cache_v_g1), (cache_k_g2, cache_v_g2)]
    att = _gather_attention(q_s.reshape(s_, N_GROUPS * H_G, HD_B),
                            kv_s.reshape(s_, 2 * N_GROUPS * H_G, HD_B), caches)
    y_sample = _post(att.reshape(1, s_, dg), xs1, ms[1], g_norm[1], w_o, w_up[1], w_dn[1], tm=s_)
    s_kv = []
    for gi in range(N_GROUPS):
        for half in range(2):
            col = (half * N_GROUPS + gi) * dg
            s_kv.append(kv_s[0, :, col:col + dg].reshape(s_, 1, H_G, HD_B))

    return (y_prompt, y_sample.reshape(s_, 1, d),
            p_c[None], p_n[None], p_m[:, 0, :H_A][None], conv_new[:, SUBLANES - (CONV_W - 1):][None],
            s_c[None], s_n[None], s_m[:, 0, :H_A][None], s_conv[None],
            *p_kv, *s_kv)
```

```python
import functools

import jax
import jax.numpy as jnp
from jax import lax
from jax.experimental import pallas as pl
from jax.experimental.pallas import tpu as pltpu

F32 = jnp.float32
BF16 = jnp.bfloat16
EPS = 1e-6
NEG_INF = float("-inf")

H_A = 4
CONV_W = 4
GROUPS_B = ((128, 1), (512, 4), (2048, 16))
N_GROUPS = len(GROUPS_B)
DILS = tuple(dil for _, dil in GROUPS_B)
H_G = 8
HD_B = 128
BAND = 128
LANES = 128
SUBLANES = 8
VMEM_LIMIT = 56 * 1024 * 1024


def _params(sem):
    return pltpu.CompilerParams(dimension_semantics=sem, vmem_limit_bytes=VMEM_LIMIT)


def _rms(x):
    return x * lax.rsqrt(jnp.mean(x * x, axis=-1, keepdims=True) + EPS)


def _sigmoid(x):
    return 0.5 * (jnp.tanh(0.5 * x) + 1.0)


def _log_sigmoid(x):
    return jnp.minimum(x, 0.0) - jnp.log1p(jnp.exp(-jnp.abs(x)))


def _resident(shape):
    nd = len(shape)
    return pl.BlockSpec(shape, lambda *_: (0,) * nd, pipeline_mode=pl.Buffered(1))


def _mod_spec(tm_mod, d, col):
    if tm_mod == 1:
        return pl.BlockSpec((1, 1, d), lambda b, i, *_: (b, 0, col))
    return pl.BlockSpec((1, tm_mod, d), lambda b, i, *_: (b, i, col))


def _ada_kernel(c_ref, w_ref, b_ref, o_ref):
    c = c_ref[...]
    s = (c * _sigmoid(c)).astype(BF16)
    o_ref[...] = jnp.dot(s, w_ref[...].astype(BF16), preferred_element_type=F32) + b_ref[...]


def _ada(c, w3, b3, layer):
    r, d = c.shape
    n = w3.shape[-1]
    tn = 1024
    return pl.pallas_call(
        _ada_kernel,
        grid=(n // tn,),
        in_specs=[pl.BlockSpec((r, d), lambda j: (0, 0)),
                  pl.BlockSpec((None, d, tn), lambda j: (layer, 0, j)),
                  pl.BlockSpec((None, 1, tn), lambda j: (layer, 0, j))],
        out_specs=pl.BlockSpec((r, tn), lambda j: (0, j)),
        out_shape=jax.ShapeDtypeStruct((r, n), F32),
        compiler_params=_params(("arbitrary",)),
        name="ada_mod",
    )(c, w3, b3)


def _proj_kernel(x_ref, g_ref, sc_ref, sh_ref, w_ref, o_ref, h_scr):
    j = pl.program_id(2)

    @pl.when(j == 0)
    def _():
        y = _rms(x_ref[0]) * g_ref[...]
        h_scr[...] = (y * (1.0 + sc_ref[0]) + sh_ref[0]).astype(BF16)

    o_ref[0] = jnp.dot(h_scr[...], w_ref[j], preferred_element_type=F32)


def _proj(x, g, mod, sc_col, sh_col, w3, *, tm, last_rows=None):
    b, t, d = x.shape
    nj, _, tn = w3.shape
    tm = min(tm, t)
    i_off = 0
    if last_rows is not None:
        assert mod.shape[1] == 1 and last_rows % tm == 0 and t % tm == 0
        i_off = (t - last_rows) // tm
        t = last_rows
    ni = t // tm
    tm_mod = 1 if mod.shape[1] == 1 else tm
    return pl.pallas_call(
        _proj_kernel,
        grid=(b, ni, nj),
        in_specs=[pl.BlockSpec((1, tm, d), lambda b_, i, j: (b_, i + i_off, 0)),
                  _resident((1, d)),
                  _mod_spec(tm_mod, d, sc_col),
                  _mod_spec(tm_mod, d, sh_col),
                  _resident((nj, d, tn))],
        out_specs=pl.BlockSpec((1, tm, tn), lambda b_, i, j: (b_, i, j)),
        out_shape=jax.ShapeDtypeStruct((b, t, nj * tn), F32),
        scratch_shapes=[pltpu.VMEM((tm, d), BF16)],
        compiler_params=_params(("arbitrary", "arbitrary", "arbitrary")),
        name="proj",
    )(x, g, mod, mod, w3)


def _proj_dil_kernel(x_ref, g_ref, sc_ref, sh_ref, w_ref, *rest, nj, out_scale, tm):
    outs = rest[:nj]
    h_scr, slab = rest[nj:]
    y = _rms(x_ref[0]) * g_ref[...]
    h = y * (1.0 + sc_ref[0]) + sh_ref[0]
    nslab = h.shape[1] // LANES
    for c in range(nslab):
        slab[c] = h[:, c * LANES:(c + 1) * LANES]
    for gi, dil in enumerate(DILS):
        if dil == 1:
            h_scr[gi] = h.astype(BF16)
            continue
        rows = tm // dil
        for r in range(dil):
            for c in range(nslab):
                h_scr[gi, r * rows:(r + 1) * rows, c * LANES:(c + 1) * LANES] = (
                    slab[c, pl.ds(r, rows, stride=dil), :].astype(BF16))
    for jj in range(nj):
        dil = DILS[jj % N_GROUPS]
        rows = tm // dil
        acc = jnp.dot(h_scr[jj % N_GROUPS], w_ref[jj], preferred_element_type=F32)
        for hh in range(H_G):
            for r in range(dil):
                blk = acc[r * rows:(r + 1) * rows, hh * HD_B:(hh + 1) * HD_B]
                if out_scale != 1.0:
                    blk = blk * out_scale
                outs[jj][0, hh, :, r * HD_B:(r + 1) * HD_B] = blk.astype(BF16)


def _proj_dil(x, g, mod, sc_col, sh_col, w3, *, tm, out_scale=1.0):
    b, t, d = x.shape
    nj, _, tn = w3.shape
    assert tn == H_G * HD_B and tm % (16 * max(DILS)) == 0 and t % tm == 0
    ni = t // tm
    in_specs = [pl.BlockSpec((1, tm, d), lambda b_, i: (b_, i, 0)),
                _resident((1, d)), _mod_spec(1, d, sc_col), _mod_spec(1, d, sh_col),
                _resident((nj, d, tn))]
    out_shape, out_specs = [], []
    for jj in range(nj):
        dil = DILS[jj % N_GROUPS]
        out_shape.append(jax.ShapeDtypeStruct((b, H_G, t // dil, dil * HD_B), BF16))
        out_specs.append(pl.BlockSpec((1, H_G, tm // dil, dil * HD_B), lambda b_, i: (b_, 0, i, 0)))
    kern = functools.partial(_proj_dil_kernel, nj=nj, out_scale=out_scale, tm=tm)
    return pl.pallas_call(
        kern,
        grid=(b, ni),
        in_specs=in_specs,
        out_specs=out_specs,
        out_shape=out_shape,
        scratch_shapes=[pltpu.VMEM((N_GROUPS, tm, d), BF16),
                        pltpu.VMEM((d // LANES, tm, LANES), F32)],
        compiler_params=_params(("arbitrary", "arbitrary")),
        name="proj_dil",
    )(x, g, mod, mod, w3)


def _inproj_kernel(x_ref, g_ref, sc_ref, sh_ref, w_ref, wg_ref, cw_ref, cb_ref, c0_ref,
                   qk_ref, v_ref, o_ref, gt_ref, cn_ref, ext_scr, *, tm, k_scale):
    i = pl.program_id(1)
    y = _rms(x_ref[0]) * g_ref[...]
    h = (y * (1.0 + sc_ref[0]) + sh_ref[0]).astype(BF16)

    @pl.when(i == 0)
    def _():
        ext_scr[0:SUBLANES, :] = c0_ref[0]

    di = w_ref.shape[2]
    for c in range(2):
        cs = slice(c * di, (c + 1) * di)
        ext_scr[SUBLANES:, cs] = jnp.dot(h, w_ref[c], preferred_element_type=F32)
        yv = cb_ref[:, cs]
        for k in range(CONV_W):
            yv = yv + cw_ref[CONV_W - 1 - k:CONV_W - k, cs] * ext_scr[SUBLANES - k:SUBLANES - k + tm, cs]
        yv = yv * _sigmoid(yv)
        if c == 1:
            yv = yv * k_scale
        qk_ref[0, :, cs] = yv.astype(BF16)
    v_ref[0] = jnp.dot(h, w_ref[2], preferred_element_type=F32).astype(BF16)
    o_ref[0] = jnp.dot(h, w_ref[3], preferred_element_type=F32)
    gt_ref[0] = jnp.dot(h, wg_ref[...], preferred_element_type=F32)
    last = ext_scr[tm:tm + SUBLANES, :]
    cn_ref[0] = last
    ext_scr[0:SUBLANES, :] = last


def _inproj_prompt(x, g, mod, w4, wg, cw, cb, conv0, *, tm):
    b, t, d = x.shape
    di = w4.shape[2]
    ni = t // tm
    dk = di // H_A
    kern = functools.partial(_inproj_kernel, tm=tm, k_scale=dk ** -0.5)
    row = lambda n: pl.BlockSpec((1, tm, n), lambda b_, i: (b_, i, 0))
    return pl.pallas_call(
        kern,
        grid=(b, ni),
        in_specs=[row(d), _resident((1, d)), _mod_spec(1, d, 1), _mod_spec(1, d, 0),
                  _resident(w4.shape), _resident(wg.shape), _resident(cw.shape), _resident(cb.shape),
                  pl.BlockSpec((1, SUBLANES, 2 * di), lambda b_, i: (b_, 0, 0))],
        out_specs=[row(2 * di), row(di), row(di), row(LANES),
                   pl.BlockSpec((1, SUBLANES, 2 * di), lambda b_, i: (b_, 0, 0))],
        out_shape=[jax.ShapeDtypeStruct((b, t, 2 * di), BF16),
                   jax.ShapeDtypeStruct((b, t, di), BF16),
                   jax.ShapeDtypeStruct((b, t, di), F32),
                   jax.ShapeDtypeStruct((b, t, LANES), F32),
                   jax.ShapeDtypeStruct((b, SUBLANES, 2 * di), F32)],
        scratch_shapes=[pltpu.VMEM((tm + SUBLANES, 2 * di), F32)],
        compiler_params=_params(("arbitrary", "arbitrary")),
        name="inproj_prompt",
    )(x, g, mod, mod, w4, wg, cw, cb, conv0)


def _mlstm_kernel(qk_ref, v_ref, o_ref, gt_ref, bg_ref, c0_ref, n0_ref, m0_ref,
                  hg_ref, c_ref, n_ref, m_ref, *, L, dk):
    i = pl.program_id(1)

    @pl.when(i == 0)
    def _():
        c_ref[...] = c0_ref[...]
        n_ref[...] = n0_ref[...]
        m_ref[...] = m0_ref[...]

    di = H_A * dk
    g = gt_ref[0] + bg_ref[...]
    lane = lax.broadcasted_iota(jnp.int32, (L, LANES), 1)
    gl = jnp.where(lane < H_A, g, _log_sigmoid(g))
    gl_t = gl.T
    r8 = gl_t[0:SUBLANES, :]
    pos = lax.broadcasted_iota(jnp.int32, (SUBLANES, L), 1)
    cs = r8
    s = 1
    while s < L:
        cs = cs + jnp.where(pos >= s, pltpu.roll(cs, s, axis=1), 0.0)
        s *= 2
    cs_t = jnp.concatenate([cs, jnp.zeros((LANES - SUBLANES, L), F32)], axis=0).T
    row = lax.broadcasted_iota(jnp.int32, (L, L), 0)
    col = lax.broadcasted_iota(jnp.int32, (L, L), 1)
    causal = col <= row
    m_all = m_ref[0]
    lane1 = lax.broadcasted_iota(jnp.int32, (1, LANES), 1)
    for h in range(H_A):
        q = qk_ref[0, :, h * dk:(h + 1) * dk]
        k = qk_ref[0, :, di + h * dk:di + (h + 1) * dk]
        v = v_ref[0, :, h * dk:(h + 1) * dk]
        b_col = cs_t[:, H_A + h:H_A + h + 1]
        b_row = cs[H_A + h:H_A + h + 1, :]
        li_row = r8[h:h + 1, :]
        li_col = gl[:, h:h + 1]
        m_prev = m_all[:, h:h + 1]
        dmat = jnp.where(causal, b_col - b_row + li_row, NEG_INF)
        inter = b_col + m_prev
        m_t = jnp.maximum(inter, jnp.max(dmat, axis=1, keepdims=True))
        w_inter = jnp.exp(inter - m_t)
        w_intra = jnp.exp(dmat - m_t)
        s_mat = lax.dot_general(q, k, (((1,), (1,)), ((), ())), preferred_element_type=F32) * w_intra
        cm = c_ref[0, h]
        nv = n_ref[0, h:h + 1, :]
        num = (w_inter * jnp.dot(q, cm.astype(BF16), preferred_element_type=F32)
               + jnp.dot(s_mat.astype(BF16), v, preferred_element_type=F32))
        qn = jnp.sum(q.astype(F32) * nv, axis=1, keepdims=True)
        den = w_inter * qn + jnp.sum(s_mat, axis=1, keepdims=True)
        hout = num / jnp.maximum(jnp.abs(den), jnp.exp(-m_t))
        hg_ref[0, :, h * dk:(h + 1) * dk] = (_sigmoid(o_ref[0, :, h * dk:(h + 1) * dk]) * hout).astype(BF16)
        m_last = m_t[L - 1:L, :]
        b_last = b_col[L - 1:L, :]
        wl_col = jnp.exp(b_last - b_col + li_col - m_last)
        w_last = w_inter[L - 1:L, :]
        kw = k.astype(F32) * wl_col
        c_ref[0, h] = w_last * cm + lax.dot_general(
            kw.astype(BF16), v, (((0,), (0,)), ((), ())), preferred_element_type=F32)
        n_ref[0, h:h + 1, :] = w_last * nv + jnp.sum(kw, axis=0, keepdims=True)
        m_all = jnp.where(lane1 == h, m_last, m_all)
    m_ref[0] = m_all


def _mlstm_prompt(qk, v, o_pre, gates, bg, c0, n0, m0, *, L):
    b, t, di = v.shape
    dk = di // H_A
    nc = t // L
    kern = functools.partial(_mlstm_kernel, L=L, dk=dk)
    row = lambda n: pl.BlockSpec((1, L, n), lambda b_, i: (b_, i, 0))
    st4 = pl.BlockSpec((1, H_A, dk, dk), lambda b_, i: (b_, 0, 0, 0))
    st3 = pl.BlockSpec((1, H_A, dk), lambda b_, i: (b_, 0, 0))
    stm = pl.BlockSpec((1, 1, LANES), lambda b_, i: (b_, 0, 0))
    return pl.pallas_call(
        kern,
        grid=(b, nc),
        in_specs=[row(2 * di), row(di), row(di), row(LANES), _resident((1, LANES)), st4, st3, stm],
        out_specs=[row(di), st4, st3, stm],
        out_shape=[jax.ShapeDtypeStruct((b, t, di), BF16),
                   jax.ShapeDtypeStruct((b, H_A, dk, dk), F32),
                   jax.ShapeDtypeStruct((b, H_A, dk), F32),
                   jax.ShapeDtypeStruct((b, 1, LANES), F32)],
        compiler_params=_params(("arbitrary", "arbitrary")),
        name="mlstm_prompt",
    )(qk, v, o_pre, gates, bg, c0, n0, m0)


def _mlstm_step_kernel(pj_ref, gt_ref, bg_ref, cw_ref, cb_ref, cv_ref, c0_ref, n0_ref, m0_ref,
                       hg_ref, c_ref, n_ref, m_ref, cvn_ref, *, dk):
    di = H_A * dk
    pj = pj_ref[0]
    u = pj[:, 0:2 * di]
    buf = cv_ref[0]
    yv = cb_ref[...] + cw_ref[CONV_W - 1:CONV_W, :] * u
    for j in range(CONV_W - 1):
        yv = yv + cw_ref[j:j + 1, :] * buf[j:j + 1, :]
    yv = yv * _sigmoid(yv)
    cvn_ref[0] = jnp.concatenate([buf[1:CONV_W - 1, :], u], axis=0)
    g = gt_ref[0] + bg_ref[...]
    m_all = m0_ref[0]
    lane1 = lax.broadcasted_iota(jnp.int32, (1, LANES), 1)
    row8 = lax.broadcasted_iota(jnp.int32, (SUBLANES, dk), 0)
    for h in range(H_A):
        q = yv[:, h * dk:(h + 1) * dk]
        k = yv[:, di + h * dk:di + (h + 1) * dk] * (dk ** -0.5)
        v = pj[:, 2 * di + h * dk:2 * di + (h + 1) * dk]
        o_pre = pj[:, 3 * di + h * dk:3 * di + (h + 1) * dk]
        li = g[:, h:h + 1]
        lf = _log_sigmoid(g[:, H_A + h:H_A + h + 1])
        m_prev = m_all[:, h:h + 1]
        inter = lf + m_prev
        m_t = jnp.maximum(inter, li)
        w_inter = jnp.exp(inter - m_t)
        w_intra = jnp.exp(li - m_t)
        cm = c0_ref[0, h]
        nv = n0_ref[0, h:h + 1, :]
        q8 = jnp.where(row8 == 0, jnp.broadcast_to(q, (SUBLANES, dk)), 0.0).astype(BF16)
        s = jnp.sum(q * k, axis=1, keepdims=True) * w_intra
        qc = jnp.dot(q8, cm.astype(BF16), preferred_element_type=F32)[0:1, :]
        num = w_inter * qc + s * v
        den = w_inter * jnp.sum(q * nv, axis=1, keepdims=True) + s
        hout = num / jnp.maximum(jnp.abs(den), jnp.exp(-m_t))
        hg_ref[0, :, h * dk:(h + 1) * dk] = _sigmoid(o_pre) * hout
        kw = k * w_intra
        kw8 = jnp.where(row8 == 0, jnp.broadcast_to(kw, (SUBLANES, dk)), 0.0).astype(BF16)
        v8 = jnp.where(row8 == 0, jnp.broadcast_to(v, (SUBLANES, dk)), 0.0).astype(BF16)
        c_ref[0, h] = w_inter * cm + lax.dot_general(
            kw8, v8, (((0,), (0,)), ((), ())), preferred_element_type=F32)
        n_ref[0, h:h + 1, :] = w_inter * nv + kw
        m_all = jnp.where(lane1 == h, m_t, m_all)
    m_ref[0] = m_all


def _mlstm_step(pj, gates, bg, cw, cb, conv, c0, n0, m0):
    s_, _, n4 = pj.shape
    di = n4 // 4
    dk = di // H_A
    kern = functools.partial(_mlstm_step_kernel, dk=dk)
    per = lambda *shape: pl.BlockSpec((1,) + shape, lambda b_: (b_,) + (0,) * len(shape))
    return pl.pallas_call(
        kern,
        grid=(s_,),
        in_specs=[per(1, n4), per(1, LANES), _resident((1, LANES)), _resident(cw.shape),
                  _resident(cb.shape), per(CONV_W - 1, 2 * di), per(H_A, dk, dk), per(H_A, dk),
                  per(1, LANES)],
        out_specs=[per(1, di), per(H_A, dk, dk), per(H_A, dk), per(1, LANES), per(CONV_W - 1, 2 * di)],
        out_shape=[jax.ShapeDtypeStruct((s_, 1, di), F32),
                   jax.ShapeDtypeStruct((s_, H_A, dk, dk), F32),
                   jax.ShapeDtypeStruct((s_, H_A, dk), F32),
                   jax.ShapeDtypeStruct((s_, 1, LANES), F32),
                   jax.ShapeDtypeStruct((s_, CONV_W - 1, 2 * di), F32)],
        compiler_params=_params(("arbitrary",)),
        name="mlstm_step",
    )(pj, gates, bg, cw, cb, conv, c0, n0, m0)


def _post_kernel(*refs, attn, ff_chunk, tm):
    if attn:
        og = refs[0:N_GROUPS]
        ls = refs[N_GROUPS:2 * N_GROUPS]
        rest = refs[2 * N_GROUPS:-1]
        nat = refs[-1]
        for gi, dil in enumerate(DILS):
            rows = tm // dil
            for h in range(H_G):
                for r in range(dil):
                    blk = og[gi][0, h, :, r * HD_B:(r + 1) * HD_B].astype(F32)
                    if dil == 1:
                        nat[gi, h] = blk
                    else:
                        nat[gi, h, pl.ds(r, rows, stride=dil), :] = blk
        l = [r[0] for r in ls]
        mx = functools.reduce(jnp.maximum, l)
        e = [jnp.exp(a - mx) for a in l]
        inv = 1.0 / functools.reduce(lambda a, b_: a + b_, e)
        w = [a * inv for a in e]
        parts = []
        for h in range(H_G):
            acc_h = w[0][:, h:h + 1] * nat[0, h]
            for gi in range(1, N_GROUPS):
                acc_h = acc_h + w[gi][:, h:h + 1] * nat[gi, h]
            parts.append(acc_h.astype(BF16))
        lhs = jnp.concatenate(parts, axis=1)
    else:
        lhs = refs[0][0].astype(BF16)
        rest = refs[1:]
    x_ref, gt1_ref, sh2_ref, sc2_ref, gt2_ref, gn_ref, wo_ref, wup_ref, wdn_ref, out_ref = rest
    a = jnp.dot(lhs, wo_ref[...], preferred_element_type=F32)
    x1 = x_ref[0] + gt1_ref[0] * (_rms(a) * gn_ref[1:2, :])
    h2 = (_rms(x1) * gn_ref[2:3, :] * (1.0 + sc2_ref[0]) + sh2_ref[0]).astype(BF16)
    dff = wup_ref.shape[1]
    acc = None
    for c in range(dff // ff_chunk):
        cs = slice(c * ff_chunk, (c + 1) * ff_chunk)
        u = jnp.maximum(jnp.dot(h2, wup_ref[:, cs], preferred_element_type=F32), 0.0)
        p = jnp.dot((u * u).astype(BF16), wdn_ref[cs, :], preferred_element_type=F32)
        acc = p if acc is None else acc + p
    out_ref[0] = x1 + gt2_ref[0] * (_rms(acc) * gn_ref[3:4, :])


def _post(lhs, x, mod, gn, wo, wup, wdn, *, tm, lses=None):
    b, t, d = x.shape
    tm = min(tm, t)
    ni = t // tm
    tm_mod = 1 if mod.shape[1] == 1 else tm
    attn = lses is not None
    row = lambda n: pl.BlockSpec((1, tm, n), lambda b_, i: (b_, i, 0))
    scratch = []
    if attn:
        lhs_specs = ([pl.BlockSpec((1, H_G, tm // dil, dil * HD_B), lambda b_, i: (b_, 0, i, 0))
                      for dil in DILS] + [row(H_G)] * N_GROUPS)
        lhs_args = list(lhs) + list(lses)
        scratch = [pltpu.VMEM((N_GROUPS, H_G, tm, HD_B), F32)]
    else:
        lhs_specs = [row(lhs.shape[-1])]
        lhs_args = [lhs]
    kern = functools.partial(_post_kernel, attn=attn, ff_chunk=1024, tm=tm)
    return pl.pallas_call(
        kern,
        grid=(b, ni),
        in_specs=lhs_specs + [row(d), _mod_spec(tm_mod, d, 2), _mod_spec(tm_mod, d, 3),
                              _mod_spec(tm_mod, d, 4), _mod_spec(tm_mod, d, 5),
                              _resident(gn.shape), _resident(wo.shape), _resident(wup.shape),
                              _resident(wdn.shape)],
        out_specs=row(d),
        out_shape=jax.ShapeDtypeStruct((b, t, d), F32),
        scratch_shapes=scratch,
        compiler_params=_params(("arbitrary", "arbitrary")),
        name="post_mlp",
    )(*lhs_args, x, mod, mod, mod, mod, gn, wo, wup, wdn)


def _band_kernel(q_ref, k_ref, v_ref, kh_ref, vh_ref, o_ref, l_ref, *, dil, nb, hps):
    n = pl.program_id(2)
    iq = lax.broadcasted_iota(jnp.int32, (BAND, 2 * BAND), 0)
    ik = lax.broadcasted_iota(jnp.int32, (BAND, 2 * BAND), 1)
    band = (ik >= iq) & (ik <= iq + BAND)
    lane = lax.broadcasted_iota(jnp.int32, (BAND, LANES), 1)

    def block(q, kw, vw, ok, r):
        s = jnp.einsum("hqd,hkd->hqk", q, kw, preferred_element_type=F32)
        s = jnp.where(ok[None], s, NEG_INF)
        m = jnp.max(s, axis=-1, keepdims=True)
        p = jnp.exp(s - m)
        den = jnp.sum(p, axis=-1, keepdims=True)
        o = jnp.einsum("hqk,hkd->hqd", p.astype(BF16), vw, preferred_element_type=F32) / den
        lse = m + jnp.log(den)
        tile = jnp.zeros((BAND, LANES), F32)
        for h in range(hps):
            tile = jnp.where(lane == r * hps + h, lse[h], tile)
        return o, tile

    for r in range(dil):
        ls = slice(r * HD_B, (r + 1) * HD_B)
        lsl = slice(r * hps, (r + 1) * hps)
        kw0 = jnp.concatenate([kh_ref[0, :, :, ls], k_ref[0, :, 0:BAND, ls]], axis=1)
        vw0 = jnp.concatenate([vh_ref[0, :, :, ls], v_ref[0, :, 0:BAND, ls]], axis=1)
        kmin = jnp.where(n == 0, BAND, 0)
        o, tile = block(q_ref[0, :, 0:BAND, ls], kw0, vw0, band & (ik >= kmin), r)
        o_ref[0, :, 0:BAND, ls] = o.astype(o_ref.dtype)
        l_ref[0, 0, 0:BAND, lsl] = tile[:, lsl]

        def body(jb, carry, r=r, ls=ls, lsl=lsl):
            r0 = pl.multiple_of(jb * BAND, BAND)
            rk = pl.multiple_of((jb - 1) * BAND, BAND)
            o, tile = block(q_ref[0, :, pl.ds(r0, BAND), ls], k_ref[0, :, pl.ds(rk, 2 * BAND), ls],
                            v_ref[0, :, pl.ds(rk, 2 * BAND), ls], band, r)
            o_ref[0, :, pl.ds(r0, BAND), ls] = o.astype(o_ref.dtype)
            l_ref[0, 0, pl.ds(r0, BAND), lsl] = tile[:, lsl]
            return carry

        if nb > 1:
            lax.fori_loop(1, nb, body, 0, unroll=3 if (nb - 1) % 3 == 0 else 1)


def _band_attention(q, k, v, grp, *, tokens_per_step=2048, hps=4):
    b, _, tsub, w = q.shape
    dil = DILS[grp]
    t = tsub * dil
    rows = tokens_per_step // dil
    nb = rows // BAND
    nsteps = tsub // rows
    nhb = H_G // hps
    cur = pl.BlockSpec((1, hps, rows, w), lambda b_, hb, n: (b_, hb, n, 0))
    halo = pl.BlockSpec((1, hps, BAND, w), lambda b_, hb, n: (b_, hb, jnp.maximum(n * nb - 1, 0), 0))
    kern = functools.partial(_band_kernel, dil=dil, nb=nb, hps=hps)
    o, l = pl.pallas_call(
        kern,
        grid=(b, nhb, nsteps),
        in_specs=[cur, cur, cur, halo, halo],
        out_specs=[cur, pl.BlockSpec((1, 1, rows, dil * hps), lambda b_, hb, n: (b_, hb, n, 0))],
        out_shape=[jax.ShapeDtypeStruct((b, H_G, tsub, w), BF16),
                   jax.ShapeDtypeStruct((b, nhb, tsub, dil * hps), F32)],
        compiler_params=_params(("arbitrary", "arbitrary", "arbitrary")),
        name="band_attn_g%d" % grp,
    )(q, k, v, k, v)
    l = l.reshape(b, nhb, tsub, dil, hps).transpose(0, 2, 3, 1, 4).reshape(b, t, H_G)
    return o, l


def _gather_attn_kernel(q_ref, kv_ref, *rest, scale):
    caches = rest[0:2 * N_GROUPS]
    o_ref = rest[2 * N_GROUPS]
    outs, lses = [], []
    for gi in range(N_GROUPS):
        kc = caches[2 * gi][0]
        vc = caches[2 * gi + 1][0]
        qg = q_ref[0, gi * H_G:(gi + 1) * H_G, :]
        kn = kv_ref[0, gi * H_G:(gi + 1) * H_G, :]
        vn = kv_ref[0, (N_GROUPS + gi) * H_G:(N_GROUPS + gi + 1) * H_G, :]
        sc = jnp.sum(kc * qg[None], axis=-1, keepdims=True) * scale
        sn = jnp.sum(kn * qg, axis=-1, keepdims=True) * scale
        m = jnp.maximum(jnp.max(sc, axis=0), sn)
        pc = jnp.exp(sc - m[None])
        pn = jnp.exp(sn - m)
        den = jnp.sum(pc, axis=0) + pn
        outs.append((jnp.sum(pc * vc, axis=0) + pn * vn) / den)
        lses.append(m + jnp.log(den))
    mx = functools.reduce(jnp.maximum, lses)
    e = [jnp.exp(a - mx) for a in lses]
    inv = 1.0 / functools.reduce(lambda a, b_: a + b_, e)
    acc = (e[0] * inv) * outs[0]
    for gi in range(1, N_GROUPS):
        acc = acc + (e[gi] * inv) * outs[gi]
    o_ref[0] = acc


def _gather_attention(q, kvn, caches):
    s_ = q.shape[0]
    args, specs = [], []
    for gi, (win, dil) in enumerate(GROUPS_B):
        for c in caches[gi]:
            wb = c.shape[1]
            assert wb == win and wb // dil == BAND
            args.append(c.reshape(s_, BAND, dil, H_G, HD_B))
            specs.append(pl.BlockSpec((1, BAND, None, H_G, HD_B), lambda b_: (b_, 0, 0, 0, 0)))
    per = lambda n: pl.BlockSpec((1, n, HD_B), lambda b_: (b_, 0, 0))
    kern = functools.partial(_gather_attn_kernel, scale=HD_B ** -0.5)
    return pl.pallas_call(
        kern,
        grid=(s_,),
        in_specs=[per(N_GROUPS * H_G), per(2 * N_GROUPS * H_G)] + specs,
        out_specs=per(H_G),
        out_shape=jax.ShapeDtypeStruct((s_, H_G, HD_B), F32),
        compiler_params=_params(("arbitrary",)),
        name="gather_attn",
    )(q, kvn, *args)


def kernel(x_prompt, x_sample, c_prompt, c_sample, state_C, state_n, state_m, state_conv,
           cache_k_g0, cache_v_g0, cache_k_g1, cache_v_g1, cache_k_g2, cache_v_g2,
           w_ada, b_ada, g_norm, w_mlp_up, w_mlp_down, w_a_in, b_a_gate, w_a_conv, b_a_conv,
           w_a_out, g_kv, w_ada_kv, b_ada_kv, w_kv, w_b_q, w_b_o):
    bp, t, d = x_prompt.shape
    s_ = x_sample.shape[0]
    assert x_sample.shape[1] == 1 and w_ada.shape[0] == 2
    di = w_a_out.shape[1]
    dk = di // H_A
    dg = H_G * HD_B
    tn = 1024

    def chunks(w, n):
        return w.astype(BF16).reshape(w.shape[0], n // tn, tn).transpose(1, 0, 2)

    w_in = w_a_in[0]
    w_in4 = chunks(w_in[:, :4 * di], 4 * di)
    w_gate = jnp.pad(w_in[:, 4 * di:], ((0, 0), (0, LANES - 2 * H_A))).astype(BF16)
    b_gate = jnp.pad(b_a_gate[0], (0, LANES - 2 * H_A)).reshape(1, LANES)
    w_q3 = chunks(w_b_q[0], N_GROUPS * dg)
    w_kv6 = chunks(w_kv, 2 * N_GROUPS * dg)
    w_out = w_a_out[0].astype(BF16)
    w_o = w_b_o[0].astype(BF16)
    w_up = w_mlp_up.astype(BF16)
    w_dn = w_mlp_down.astype(BF16)
    cw = w_a_conv[0]
    cb = b_a_conv[0].reshape(1, 2 * di)
    g_kv2 = g_kv.reshape(1, d)

    c_all = jnp.concatenate([c_sample, c_prompt], axis=0)
    b_ada3 = b_ada.reshape(2, 1, 6 * d)
    mods = [_ada(c_all, w_ada, b_ada3, layer) for layer in range(2)]
    mod_kv = _ada(c_all, w_ada_kv.reshape(1, d, 2 * d), b_ada_kv.reshape(1, 1, 2 * d), 0)
    mp = [m[s_:].reshape(bp, 1, -1) for m in mods + [mod_kv]]
    ms = [m[:s_].reshape(1, s_, -1) for m in mods + [mod_kv]]

    conv0 = jnp.zeros((bp, SUBLANES, 2 * di), F32)
    qk, v, o_pre, gates, conv_new = _inproj_prompt(
        x_prompt, g_norm[0, 0:1], mp[0], w_in4, w_gate, cw, cb, conv0, tm=512)
    hg, p_c, p_n, p_m = _mlstm_prompt(
        qk, v, o_pre, gates, b_gate,
        jnp.zeros((bp, H_A, dk, dk), F32), jnp.zeros((bp, H_A, dk), F32),
        jnp.zeros((bp, 1, LANES), F32), L=256)
    x1 = _post(hg, x_prompt, mp[0], g_norm[0], w_out, w_up[0], w_dn[0], tm=512)
    max_win = max(wn for wn, _ in GROUPS_B)
    kv_d = _proj_dil(x1, g_kv2, mp[2], 1, 0, w_kv6, tm=512)
    kv_tail = _proj(x1, g_kv2, mp[2], 1, 0, w_kv6, tm=1024, last_rows=max_win)
    q_d = _proj_dil(x1, g_norm[1, 0:1], mp[1], 1, 0, w_q3, tm=512, out_scale=HD_B ** -0.5)
    outs, lses = [], []
    for gi in range(N_GROUPS):
        o_g, l_g = _band_attention(q_d[gi], kv_d[gi], kv_d[N_GROUPS + gi], gi)
        outs.append(o_g)
        lses.append(l_g)
    y_prompt = _post(outs, x1, mp[1], g_norm[1], w_o, w_up[1], w_dn[1], tm=512, lses=lses)
    p_kv = []
    for gi, (win, _) in enumerate(GROUPS_B):
        for half in range(2):
            col = (half * N_GROUPS + gi) * dg
            p_kv.append(kv_tail[:, max_win - win:, col:col + dg].reshape(bp, win, H_G, HD_B))

    xs = x_sample.reshape(1, s_, d)
    pj = _proj(xs, g_norm[0, 0:1], ms[0], 1, 0, w_in4, tm=s_)
    gts = _proj(xs, g_norm[0, 0:1], ms[0], 1, 0, w_gate.reshape(1, d, LANES), tm=s_)
    m0 = jnp.pad(state_m[0], ((0, 0), (0, LANES - H_A))).reshape(s_, 1, LANES)
    hgs, s_c, s_n, s_m, s_conv = _mlstm_step(
        pj.reshape(s_, 1, 4 * di), gts.reshape(s_, 1, LANES), b_gate, cw, cb,
        state_conv[0], state_C[0], state_n[0], m0)
    xs1 = _post(hgs.reshape(1, s_, di), xs, ms[0], g_norm[0], w_out, w_up[0], w_dn[0], tm=s_)
    kv_s = _proj(xs1, g_kv2, ms[2], 1, 0, w_kv6, tm=s_)
    q_s = _proj(xs1, g_norm[1, 0:1], ms[1], 1, 0, w_q3, tm=s_)
    caches = [(cache_k_g0, cache_v_g0), (cache_k_g1, cache_v_g1), (cache_k_g2, cache_v_g2)]
    att = _gather_attention(q_s.reshape(s_, N_GROUPS * H_G, HD_B),
                            kv_s.reshape(s_, 2 * N_GROUPS * H_G, HD_B), caches)
    y_sample = _post(att.reshape(1, s_, dg), xs1, ms[1], g_norm[1], w_o, w_up[1], w_dn[1], tm=s_)
    s_kv = []
    for gi in range(N_GROUPS):
        for half in range(2):
            col = (half * N_GROUPS + gi) * dg
            s_kv.append(kv_s[0, :, col:col + dg].reshape(s_, 1, H_G, HD_B))

    return (y_prompt, y_sample.reshape(s_, 1, d),
            p_c[None], p_n[None], p_m[:, 0, :H_A][None], conv_new[:, SUBLANES - (CONV_W - 1):][None],
            s_c[None], s_n[None], s_m[:, 0, :H_A][None], s_conv[None],
            *p_kv, *s_kv)
```

```python
import functools

import jax
import jax.numpy as jnp
from jax import lax
from jax.experimental import pallas as pl
from jax.experimental.pallas import tpu as pltpu

F32 = jnp.float32
BF16 = jnp.bfloat16
EPS = 1e-6
NEG_INF = float("-inf")

H_A = 4
CONV_W = 4
GROUPS_B = ((128, 1), (512, 4), (2048, 16))
N_GROUPS = len(GROUPS_B)
DILS = tuple(dil for _, dil in GROUPS_B)
H_G = 8
HD_B = 128
BAND = 128
LANES = 128
SUBLANES = 8
VMEM_LIMIT = 56 * 1024 * 1024


def _params(sem):
    return pltpu.CompilerParams(dimension_semantics=sem, vmem_limit_bytes=VMEM_LIMIT)


def _rms(x):
    return x * lax.rsqrt(jnp.mean(x * x, axis=-1, keepdims=True) + EPS)


def _sigmoid(x):
    return 0.5 * (jnp.tanh(0.5 * x) + 1.0)


def _log_sigmoid(x):
    return jnp.minimum(x, 0.0) - jnp.log1p(jnp.exp(-jnp.abs(x)))


def _resident(shape):
    nd = len(shape)
    return pl.BlockSpec(shape, lambda *_: (0,) * nd, pipeline_mode=pl.Buffered(1))


def _mod_spec(tm_mod, d, col):
    if tm_mod == 1:
        return pl.BlockSpec((1, 1, d), lambda b, i, *_: (b, 0, col))
    return pl.BlockSpec((1, tm_mod, d), lambda b, i, *_: (b, i, col))


def _ada_kernel(c_ref, w_ref, b_ref, o_ref):
    c = c_ref[...]
    s = (c * _sigmoid(c)).astype(BF16)
    o_ref[...] = jnp.dot(s, w_ref[...].astype(BF16), preferred_element_type=F32) + b_ref[...]


def _ada(c, w3, b3, layer):
    r, d = c.shape
    n = w3.shape[-1]
    tn = 1024
    return pl.pallas_call(
        _ada_kernel,
        grid=(n // tn,),
        in_specs=[pl.BlockSpec((r, d), lambda j: (0, 0)),
                  pl.BlockSpec((None, d, tn), lambda j: (layer, 0, j)),
                  pl.BlockSpec((None, 1, tn), lambda j: (layer, 0, j))],
        out_specs=pl.BlockSpec((r, tn), lambda j: (0, j)),
        out_shape=jax.ShapeDtypeStruct((r, n), F32),
        compiler_params=_params(("arbitrary",)),
        name="ada_mod",
    )(c, w3, b3)


def _proj_kernel(x_ref, g_ref, sc_ref, sh_ref, w_ref, o_ref, h_scr):
    j = pl.program_id(2)

    @pl.when(j == 0)
    def _():
        y = _rms(x_ref[0]) * g_ref[...]
        h_scr[...] = (y * (1.0 + sc_ref[0]) + sh_ref[0]).astype(BF16)

    o_ref[0] = jnp.dot(h_scr[...], w_ref[j], preferred_element_type=F32)


def _proj(x, g, mod, sc_col, sh_col, w3, *, tm, last_rows=None):
    b, t, d = x.shape
    nj, _, tn = w3.shape
    tm = min(tm, t)
    i_off = 0
    if last_rows is not None:
        assert mod.shape[1] == 1 and last_rows % tm == 0 and t % tm == 0
        i_off = (t - last_rows) // tm
        t = last_rows
    ni = t // tm
    tm_mod = 1 if mod.shape[1] == 1 else tm
    return pl.pallas_call(
        _proj_kernel,
        grid=(b, ni, nj),
        in_specs=[pl.BlockSpec((1, tm, d), lambda b_, i, j: (b_, i + i_off, 0)),
                  _resident((1, d)),
                  _mod_spec(tm_mod, d, sc_col),
                  _mod_spec(tm_mod, d, sh_col),
                  _resident((nj, d, tn))],
        out_specs=pl.BlockSpec((1, tm, tn), lambda b_, i, j: (b_, i, j)),
        out_shape=jax.ShapeDtypeStruct((b, t, nj * tn), F32),
        scratch_shapes=[pltpu.VMEM((tm, d), BF16)],
        compiler_params=_params(("arbitrary", "arbitrary", "arbitrary")),
        name="proj",
    )(x, g, mod, mod, w3)


def _proj_dil_kernel(x_ref, g_ref, sc_ref, sh_ref, w_ref, *rest, nj, out_scale, tm):
    outs = rest[:nj]
    h_scr, slab = rest[nj:]
    y = _rms(x_ref[0]) * g_ref[...]
    h = y * (1.0 + sc_ref[0]) + sh_ref[0]
    nslab = h.shape[1] // LANES
    for c in range(nslab):
        slab[c] = h[:, c * LANES:(c + 1) * LANES]
    for gi, dil in enumerate(DILS):
        if dil == 1:
            h_scr[gi] = h.astype(BF16)
            continue
        rows = tm // dil
        for r in range(dil):
            for c in range(nslab):
                h_scr[gi, r * rows:(r + 1) * rows, c * LANES:(c + 1) * LANES] = (
                    slab[c, pl.ds(r, rows, stride=dil), :].astype(BF16))
    for jj in range(nj):
        dil = DILS[jj % N_GROUPS]
        rows = tm // dil
        acc = jnp.dot(h_scr[jj % N_GROUPS], w_ref[jj], preferred_element_type=F32)
        for hh in range(H_G):
            for r in range(dil):
                blk = acc[r * rows:(r + 1) * rows, hh * HD_B:(hh + 1) * HD_B]
                if out_scale != 1.0:
                    blk = blk * out_scale
                outs[jj][0, hh, :, r * HD_B:(r + 1) * HD_B] = blk.astype(BF16)


def _proj_dil(x, g, mod, sc_col, sh_col, w3, *, tm, out_scale=1.0):
    b, t, d = x.shape
    nj, _, tn = w3.shape
    assert tn == H_G * HD_B and tm % (16 * max(DILS)) == 0 and t % tm == 0
    ni = t // tm
    in_specs = [pl.BlockSpec((1, tm, d), lambda b_, i: (b_, i, 0)),
                _resident((1, d)), _mod_spec(1, d, sc_col), _mod_spec(1, d, sh_col),
                _resident((nj, d, tn))]
    out_shape, out_specs = [], []
    for jj in range(nj):
        dil = DILS[jj % N_GROUPS]
        out_shape.append(jax.ShapeDtypeStruct((b, H_G, t // dil, dil * HD_B), BF16))
        out_specs.append(pl.BlockSpec((1, H_G, tm // dil, dil * HD_B), lambda b_, i: (b_, 0, i, 0)))
    kern = functools.partial(_proj_dil_kernel, nj=nj, out_scale=out_scale, tm=tm)
    return pl.pallas_call(
        kern,
        grid=(b, ni),
        in_specs=in_specs,
        out_specs=out_specs,
        out_shape=out_shape,
        scratch_shapes=[pltpu.VMEM((N_GROUPS, tm, d), BF16),
                        pltpu.VMEM((d // LANES, tm, LANES), F32)],
        compiler_params=_params(("arbitrary", "arbitrary")),
        name="proj_dil",
    )(x, g, mod, mod, w3)


def _inproj_kernel(x_ref, g_ref, sc_ref, sh_ref, w_ref, wg_ref, cw_ref, cb_ref, c0_ref,
                   qk_ref, v_ref, o_ref, gt_ref, cn_ref, ext_scr, *, tm, k_scale):
    i = pl.program_id(1)
    y = _rms(x_ref[0]) * g_ref[...]
    h = (y * (1.0 + sc_ref[0]) + sh_ref[0]).astype(BF16)

    @pl.when(i == 0)
    def _():
        ext_scr[0:SUBLANES, :] = c0_ref[0]

    di = w_ref.shape[2]
    for c in range(2):
        cs = slice(c * di, (c + 1) * di)
        ext_scr[SUBLANES:, cs] = jnp.dot(h, w_ref[c], preferred_element_type=F32)
        yv = cb_ref[:, cs]
        for k in range(CONV_W):
            yv = yv + cw_ref[CONV_W - 1 - k:CONV_W - k, cs] * ext_scr[SUBLANES - k:SUBLANES - k + tm, cs]
        yv = yv * _sigmoid(yv)
        if c == 1:
            yv = yv * k_scale
        qk_ref[0, :, cs] = yv.astype(BF16)
    v_ref[0] = jnp.dot(h, w_ref[2], preferred_element_type=F32).astype(BF16)
    o_ref[0] = jnp.dot(h, w_ref[3], preferred_element_type=F32)
    gt_ref[0] = jnp.dot(h, wg_ref[...], preferred_element_type=F32)
    last = ext_scr[tm:tm + SUBLANES, :]
    cn_ref[0] = last
    ext_scr[0:SUBLANES, :] = last


def _inproj_prompt(x, g, mod, w4, wg, cw, cb, conv0, *, tm):
    b, t, d = x.shape
    di = w4.shape[2]
    ni = t // tm
    dk = di // H_A
    kern = functools.partial(_inproj_kernel, tm=tm, k_scale=dk ** -0.5)
    row = lambda n: pl.BlockSpec((1, tm, n), lambda b_, i: (b_, i, 0))
    return pl.pallas_call(
        kern,
        grid=(b, ni),
        in_specs=[row(d), _resident((1, d)), _mod_spec(1, d, 1), _mod_spec(1, d, 0),
                  _resident(w4.shape), _resident(wg.shape), _resident(cw.shape), _resident(cb.shape),
                  pl.BlockSpec((1, SUBLANES, 2 * di), lambda b_, i: (b_, 0, 0))],
        out_specs=[row(2 * di), row(di), row(di), row(LANES),
                   pl.BlockSpec((1, SUBLANES, 2 * di), lambda b_, i: (b_, 0, 0))],
        out_shape=[jax.ShapeDtypeStruct((b, t, 2 * di), BF16),
                   jax.ShapeDtypeStruct((b, t, di), BF16),
                   jax.ShapeDtypeStruct((b, t, di), F32),
                   jax.ShapeDtypeStruct((b, t, LANES), F32),
                   jax.ShapeDtypeStruct((b, SUBLANES, 2 * di), F32)],
        scratch_shapes=[pltpu.VMEM((tm + SUBLANES, 2 * di), F32)],
        compiler_params=_params(("arbitrary", "arbitrary")),
        name="inproj_prompt",
    )(x, g, mod, mod, w4, wg, cw, cb, conv0)


def _mlstm_kernel(qk_ref, v_ref, o_ref, gt_ref, bg_ref, c0_ref, n0_ref, m0_ref,
                  hg_ref, c_ref, n_ref, m_ref, *, L, dk):
    i = pl.program_id(1)

    @pl.when(i == 0)
    def _():
        c_ref[...] = c0_ref[...]
        n_ref[...] = n0_ref[...]
        m_ref[...] = m0_ref[...]

    di = H_A * dk
    g = gt_ref[0] + bg_ref[...]
    lane = lax.broadcasted_iota(jnp.int32, (L, LANES), 1)
    gl = jnp.where(lane < H_A, g, _log_sigmoid(g))
    gl_t = gl.T
    r8 = gl_t[0:SUBLANES, :]
    pos = lax.broadcasted_iota(jnp.int32, (SUBLANES, L), 1)
    cs = r8
    s = 1
    while s < L:
        cs = cs + jnp.where(pos >= s, pltpu.roll(cs, s, axis=1), 0.0)
        s *= 2
    cs_t = jnp.concatenate([cs, jnp.zeros((LANES - SUBLANES, L), F32)], axis=0).T
    row = lax.broadcasted_iota(jnp.int32, (L, L), 0)
    col = lax.broadcasted_iota(jnp.int32, (L, L), 1)
    causal = col <= row
    m_all = m_ref[0]
    lane1 = lax.broadcasted_iota(jnp.int32, (1, LANES), 1)
    for h in range(H_A):
        q = qk_ref[0, :, h * dk:(h + 1) * dk]
        k = qk_ref[0, :, di + h * dk:di + (h + 1) * dk]
        v = v_ref[0, :, h * dk:(h + 1) * dk]
        b_col = cs_t[:, H_A + h:H_A + h + 1]
        b_row = cs[H_A + h:H_A + h + 1, :]
        li_row = r8[h:h + 1, :]
        li_col = gl[:, h:h + 1]
        m_prev = m_all[:, h:h + 1]
        dmat = jnp.where(causal, b_col - b_row + li_row, NEG_INF)
        inter = b_col + m_prev
        m_t = jnp.maximum(inter, jnp.max(dmat, axis=1, keepdims=True))
        w_inter = jnp.exp(inter - m_t)
        w_intra = jnp.exp(dmat - m_t)
        s_mat = lax.dot_general(q, k, (((1,), (1,)), ((), ())), preferred_element_type=F32) * w_intra
        cm = c_ref[0, h]
        nv = n_ref[0, h:h + 1, :]
        num = (w_inter * jnp.dot(q, cm.astype(BF16), preferred_element_type=F32)
               + jnp.dot(s_mat.astype(BF16), v, preferred_element_type=F32))
        qn = jnp.sum(q.astype(F32) * nv, axis=1, keepdims=True)
        den = w_inter * qn + jnp.sum(s_mat, axis=1, keepdims=True)
        hout = num / jnp.maximum(jnp.abs(den), jnp.exp(-m_t))
        hg_ref[0, :, h * dk:(h + 1) * dk] = (_sigmoid(o_ref[0, :, h * dk:(h + 1) * dk]) * hout).astype(BF16)
        m_last = m_t[L - 1:L, :]
        b_last = b_col[L - 1:L, :]
        wl_col = jnp.exp(b_last - b_col + li_col - m_last)
        w_last = w_inter[L - 1:L, :]
        kw = k.astype(F32) * wl_col
        c_ref[0, h] = w_last * cm + lax.dot_general(
            kw.astype(BF16), v, (((0,), (0,)), ((), ())), preferred_element_type=F32)
        n_ref[0, h:h + 1, :] = w_last * nv + jnp.sum(kw, axis=0, keepdims=True)
        m_all = jnp.where(lane1 == h, m_last, m_all)
    m_ref[0] = m_all


def _mlstm_prompt(qk, v, o_pre, gates, bg, c0, n0, m0, *, L):
    b, t, di = v.shape
    dk = di // H_A
    nc = t // L
    kern = functools.partial(_mlstm_kernel, L=L, dk=dk)
    row = lambda n: pl.BlockSpec((1, L, n), lambda b_, i: (b_, i, 0))
    st4 = pl.BlockSpec((1, H_A, dk, dk), lambda b_, i: (b_, 0, 0, 0))
    st3 = pl.BlockSpec((1, H_A, dk), lambda b_, i: (b_, 0, 0))
    stm = pl.BlockSpec((1, 1, LANES), lambda b_, i: (b_, 0, 0))
    return pl.pallas_call(
        kern,
        grid=(b, nc),
        in_specs=[row(2 * di), row(di), row(di), row(LANES), _resident((1, LANES)), st4, st3, stm],
        out_specs=[row(di), st4, st3, stm],
        out_shape=[jax.ShapeDtypeStruct((b, t, di), BF16),
                   jax.ShapeDtypeStruct((b, H_A, dk, dk), F32),
                   jax.ShapeDtypeStruct((b, H_A, dk), F32),
                   jax.ShapeDtypeStruct((b, 1, LANES), F32)],
        compiler_params=_params(("arbitrary", "arbitrary")),
        name="mlstm_prompt",
    )(qk, v, o_pre, gates, bg, c0, n0, m0)


def _mlstm_step_kernel(pj_ref, gt_ref, bg_ref, cw_ref, cb_ref, cv_ref, c0_ref, n0_ref, m0_ref,
                       hg_ref, c_ref, n_ref, m_ref, cvn_ref, *, dk):
    di = H_A * dk
    pj = pj_ref[0]
    u = pj[:, 0:2 * di]
    buf = cv_ref[0]
    yv = cb_ref[...] + cw_ref[CONV_W - 1:CONV_W, :] * u
    for j in range(CONV_W - 1):
        yv = yv + cw_ref[j:j + 1, :] * buf[j:j + 1, :]
    yv = yv * _sigmoid(yv)
    cvn_ref[0] = jnp.concatenate([buf[1:CONV_W - 1, :], u], axis=0)
    g = gt_ref[0] + bg_ref[...]
    m_all = m0_ref[0]
    lane1 = lax.broadcasted_iota(jnp.int32, (1, LANES), 1)
    row8 = lax.broadcasted_iota(jnp.int32, (SUBLANES, dk), 0)
    for h in range(H_A):
        q = yv[:, h * dk:(h + 1) * dk]
        k = yv[:, di + h * dk:di + (h + 1) * dk] * (dk ** -0.5)
        v = pj[:, 2 * di + h * dk:2 * di + (h + 1) * dk]
        o_pre = pj[:, 3 * di + h * dk:3 * di + (h + 1) * dk]
        li = g[:, h:h + 1]
        lf = _log_sigmoid(g[:, H_A + h:H_A + h + 1])
        m_prev = m_all[:, h:h + 1]
        inter = lf + m_prev
        m_t = jnp.maximum(inter, li)
        w_inter = jnp.exp(inter - m_t)
        w_intra = jnp.exp(li - m_t)
        cm = c0_ref[0, h]
        nv = n0_ref[0, h:h + 1, :]
        q8 = jnp.where(row8 == 0, jnp.broadcast_to(q, (SUBLANES, dk)), 0.0).astype(BF16)
        s = jnp.sum(q * k, axis=1, keepdims=True) * w_intra
        qc = jnp.dot(q8, cm.astype(BF16), preferred_element_type=F32)[0:1, :]
        num = w_inter * qc + s * v
        den = w_inter * jnp.sum(q * nv, axis=1, keepdims=True) + s
        hout = num / jnp.maximum(jnp.abs(den), jnp.exp(-m_t))
        hg_ref[0, :, h * dk:(h + 1) * dk] = _sigmoid(o_pre) * hout
        kw = k * w_intra
        kw8 = jnp.where(row8 == 0, jnp.broadcast_to(kw, (SUBLANES, dk)), 0.0).astype(BF16)
        v8 = jnp.where(row8 == 0, jnp.broadcast_to(v, (SUBLANES, dk)), 0.0).astype(BF16)
        c_ref[0, h] = w_inter * cm + lax.dot_general(
            kw8, v8, (((0,), (0,)), ((), ())), preferred_element_type=F32)
        n_ref[0, h:h + 1, :] = w_inter * nv + kw
        m_all = jnp.where(lane1 == h, m_t, m_all)
    m_ref[0] = m_all


def _mlstm_step(pj, gates, bg, cw, cb, conv, c0, n0, m0):
    s_, _, n4 = pj.shape
    di = n4 // 4
    dk = di // H_A
    kern = functools.partial(_mlstm_step_kernel, dk=dk)
    per = lambda *shape: pl.BlockSpec((1,) + shape, lambda b_: (b_,) + (0,) * len(shape))
    return pl.pallas_call(
        kern,
        grid=(s_,),
        in_specs=[per(1, n4), per(1, LANES), _resident((1, LANES)), _resident(cw.shape),
                  _resident(cb.shape), per(CONV_W - 1, 2 * di), per(H_A, dk, dk), per(H_A, dk),
                  per(1, LANES)],
        out_specs=[per(1, di), per(H_A, dk, dk), per(H_A, dk), per(1, LANES), per(CONV_W - 1, 2 * di)],
        out_shape=[jax.ShapeDtypeStruct((s_, 1, di), F32),
                   jax.ShapeDtypeStruct((s_, H_A, dk, dk), F32),
                   jax.ShapeDtypeStruct((s_, H_A, dk), F32),
                   jax.ShapeDtypeStruct((s_, 1, LANES), F32),
                   jax.ShapeDtypeStruct((s_, CONV_W - 1, 2 * di), F32)],
        compiler_params=_params(("arbitrary",)),
        name="mlstm_step",
    )(pj, gates, bg, cw, cb, conv, c0, n0, m0)


def _post_kernel(*refs, attn, ff_chunk, tm):
    if attn:
        og = refs[0:N_GROUPS]
        ls = refs[N_GROUPS:2 * N_GROUPS]
        rest = refs[2 * N_GROUPS:-2]
        nat, natl = refs[-2:]
        nhb = natl.shape[1]
        hps = H_G // nhb
        for gi, dil in enumerate(DILS):
            rows = tm // dil
            for r in range(dil):
                dst = slice(None) if dil == 1 else pl.ds(r, rows, stride=dil)
                for h in range(H_G):
                    nat[gi, h, dst, :] = og[gi][0, h, :, r * HD_B:(r + 1) * HD_B].astype(F32)
                for hb in range(nhb):
                    natl[gi, hb, dst, :] = ls[gi][0, hb, :, r * LANES:(r + 1) * LANES]
        w = []
        for hb in range(nhb):
            l = [natl[gi, hb] for gi in range(N_GROUPS)]
            mx = functools.reduce(jnp.maximum, l)
            e = [jnp.exp(a - mx) for a in l]
            inv = 1.0 / functools.reduce(lambda a, b_: a + b_, e)
            w.append([a * inv for a in e])
        parts = []
        for h in range(H_G):
            hb, hl = divmod(h, hps)
            acc_h = w[hb][0][:, hl:hl + 1] * nat[0, h]
            for gi in range(1, N_GROUPS):
                acc_h = acc_h + w[hb][gi][:, hl:hl + 1] * nat[gi, h]
            parts.append(acc_h.astype(BF16))
        lhs = jnp.concatenate(parts, axis=1)
    else:
        lhs = refs[0][0].astype(BF16)
        rest = refs[1:]
    x_ref, gt1_ref, sh2_ref, sc2_ref, gt2_ref, gn_ref, wo_ref, wup_ref, wdn_ref, out_ref = rest
    a = jnp.dot(lhs, wo_ref[...], preferred_element_type=F32)
    x1 = x_ref[0] + gt1_ref[0] * (_rms(a) * gn_ref[1:2, :])
    h2 = (_rms(x1) * gn_ref[2:3, :] * (1.0 + sc2_ref[0]) + sh2_ref[0]).astype(BF16)
    dff = wup_ref.shape[1]
    acc = None
    for c in range(dff // ff_chunk):
        cs = slice(c * ff_chunk, (c + 1) * ff_chunk)
        u = jnp.maximum(jnp.dot(h2, wup_ref[:, cs], preferred_element_type=F32), 0.0)
        p = jnp.dot((u * u).astype(BF16), wdn_ref[cs, :], preferred_element_type=F32)
        acc = p if acc is None else acc + p
    out_ref[0] = x1 + gt2_ref[0] * (_rms(acc) * gn_ref[3:4, :])


def _post(lhs, x, mod, gn, wo, wup, wdn, *, tm, lses=None):
    b, t, d = x.shape
    tm = min(tm, t)
    ni = t // tm
    tm_mod = 1 if mod.shape[1] == 1 else tm
    attn = lses is not None
    row = lambda n: pl.BlockSpec((1, tm, n), lambda b_, i: (b_, i, 0))
    scratch = []
    if attn:
        nhb = lses[0].shape[1]
        lhs_specs = ([pl.BlockSpec((1, H_G, tm // dil, dil * HD_B), lambda b_, i: (b_, 0, i, 0))
                      for dil in DILS]
                     + [pl.BlockSpec((1, nhb, tm // dil, dil * LANES), lambda b_, i: (b_, 0, i, 0))
                        for dil in DILS])
        lhs_args = list(lhs) + list(lses)
        scratch = [pltpu.VMEM((N_GROUPS, H_G, tm, HD_B), F32),
                   pltpu.VMEM((N_GROUPS, nhb, tm, LANES), F32)]
    else:
        lhs_specs = [row(lhs.shape[-1])]
        lhs_args = [lhs]
    kern = functools.partial(_post_kernel, attn=attn, ff_chunk=1024, tm=tm)
    return pl.pallas_call(
        kern,
        grid=(b, ni),
        in_specs=lhs_specs + [row(d), _mod_spec(tm_mod, d, 2), _mod_spec(tm_mod, d, 3),
                              _mod_spec(tm_mod, d, 4), _mod_spec(tm_mod, d, 5),
                              _resident(gn.shape), _resident(wo.shape), _resident(wup.shape),
                              _resident(wdn.shape)],
        out_specs=row(d),
        out_shape=jax.ShapeDtypeStruct((b, t, d), F32),
        scratch_shapes=scratch,
        compiler_params=_params(("arbitrary", "arbitrary")),
        name="post_mlp",
    )(*lhs_args, x, mod, mod, mod, mod, gn, wo, wup, wdn)


def _band_kernel(q_ref, k_ref, v_ref, kh_ref, vh_ref, o_ref, l_ref, *, dil, nb, hps):
    n = pl.program_id(2)
    iq = lax.broadcasted_iota(jnp.int32, (BAND, 2 * BAND), 0)
    ik = lax.broadcasted_iota(jnp.int32, (BAND, 2 * BAND), 1)
    band = (ik >= iq) & (ik <= iq + BAND)
    lane = lax.broadcasted_iota(jnp.int32, (BAND, LANES), 1)

    def block(q, kw, vw, ok, r):
        s = jnp.einsum("hqd,hkd->hqk", q, kw, preferred_element_type=F32)
        s = jnp.where(ok[None], s, NEG_INF)
        m = jnp.max(s, axis=-1, keepdims=True)
        p = jnp.exp(s - m)
        den = jnp.sum(p, axis=-1, keepdims=True)
        o = jnp.einsum("hqk,hkd->hqd", p.astype(BF16), vw, preferred_element_type=F32) / den
        lse = m + jnp.log(den)
        tile = jnp.zeros((BAND, LANES), F32)
        for h in range(hps):
            tile = jnp.where(lane == h, lse[h], tile)
        return o, tile

    for r in range(dil):
        ls = slice(r * HD_B, (r + 1) * HD_B)
        kw0 = jnp.concatenate([kh_ref[0, :, :, ls], k_ref[0, :, 0:BAND, ls]], axis=1)
        vw0 = jnp.concatenate([vh_ref[0, :, :, ls], v_ref[0, :, 0:BAND, ls]], axis=1)
        kmin = jnp.where(n == 0, BAND, 0)
        o, tile = block(q_ref[0, :, 0:BAND, ls], kw0, vw0, band & (ik >= kmin), r)
        o_ref[0, :, 0:BAND, ls] = o.astype(o_ref.dtype)
        l_ref[0, 0, 0:BAND, ls] = tile

        def body(jb, carry, r=r, ls=ls):
            r0 = pl.multiple_of(jb * BAND, BAND)
            rk = pl.multiple_of((jb - 1) * BAND, BAND)
            o, tile = block(q_ref[0, :, pl.ds(r0, BAND), ls], k_ref[0, :, pl.ds(rk, 2 * BAND), ls],
                            v_ref[0, :, pl.ds(rk, 2 * BAND), ls], band, r)
            o_ref[0, :, pl.ds(r0, BAND), ls] = o.astype(o_ref.dtype)
            l_ref[0, 0, pl.ds(r0, BAND), ls] = tile
            return carry

        if nb > 1:
            lax.fori_loop(1, nb, body, 0, unroll=3 if (nb - 1) % 3 == 0 else 1)


def _band_attention(q, k, v, grp, *, tokens_per_step=2048, hps=4):
    b, _, tsub, w = q.shape
    dil = DILS[grp]
    rows = tokens_per_step // dil
    nb = rows // BAND
    nsteps = tsub // rows
    nhb = H_G // hps
    cur = pl.BlockSpec((1, hps, rows, w), lambda b_, hb, n: (b_, hb, n, 0))
    halo = pl.BlockSpec((1, hps, BAND, w), lambda b_, hb, n: (b_, hb, jnp.maximum(n * nb - 1, 0), 0))
    kern = functools.partial(_band_kernel, dil=dil, nb=nb, hps=hps)
    return pl.pallas_call(
        kern,
        grid=(b, nhb, nsteps),
        in_specs=[cur, cur, cur, halo, halo],
        out_specs=[cur, pl.BlockSpec((1, 1, rows, w), lambda b_, hb, n: (b_, hb, n, 0))],
        out_shape=[jax.ShapeDtypeStruct((b, H_G, tsub, w), BF16),
                   jax.ShapeDtypeStruct((b, nhb, tsub, w), F32)],
        compiler_params=_params(("arbitrary", "arbitrary", "arbitrary")),
        name="band_attn_g%d" % grp,
    )(q, k, v, k, v)


def _gather_attn_kernel(q_ref, kv_ref, *rest, scale):
    caches = rest[0:2 * N_GROUPS]
    o_ref = rest[2 * N_GROUPS]
    outs, lses = [], []
    for gi in range(N_GROUPS):
        kc = caches[2 * gi][0]
        vc = caches[2 * gi + 1][0]
        qg = q_ref[0, gi * H_G:(gi + 1) * H_G, :]
        kn = kv_ref[0, gi * H_G:(gi + 1) * H_G, :]
        vn = kv_ref[0, (N_GROUPS + gi) * H_G:(N_GROUPS + gi + 1) * H_G, :]
        sc = jnp.sum(kc * qg[None], axis=-1, keepdims=True) * scale
        sn = jnp.sum(kn * qg, axis=-1, keepdims=True) * scale
        m = jnp.maximum(jnp.max(sc, axis=0), sn)
        pc = jnp.exp(sc - m[None])
        pn = jnp.exp(sn - m)
        den = jnp.sum(pc, axis=0) + pn
        outs.append((jnp.sum(pc * vc, axis=0) + pn * vn) / den)
        lses.append(m + jnp.log(den))
    mx = functools.reduce(jnp.maximum, lses)
    e = [jnp.exp(a - mx) for a in lses]
    inv = 1.0 / functools.reduce(lambda a, b_: a + b_, e)
    acc = (e[0] * inv) * outs[0]
    for gi in range(1, N_GROUPS):
        acc = acc + (e[gi] * inv) * outs[gi]
    o_ref[0] = acc


def _gather_attention(q, kvn, caches):
    s_ = q.shape[0]
    args, specs = [], []
    for gi, (win, dil) in enumerate(GROUPS_B):
        for c in caches[gi]:
            wb = c.shape[1]
            assert wb == win and wb // dil == BAND
            args.append(c.reshape(s_, BAND, dil, H_G, HD_B))
            specs.append(pl.BlockSpec((1, BAND, None, H_G, HD_B), lambda b_: (b_, 0, 0, 0, 0)))
    per = lambda n: pl.BlockSpec((1, n, HD_B), lambda b_: (b_, 0, 0))
    kern = functools.partial(_gather_attn_kernel, scale=HD_B ** -0.5)
    return pl.pallas_call(
        kern,
        grid=(s_,),
        in_specs=[per(N_GROUPS * H_G), per(2 * N_GROUPS * H_G)] + specs,
        out_specs=per(H_G),
        out_shape=jax.ShapeDtypeStruct((s_, H_G, HD_B), F32),
        compiler_params=_params(("arbitrary",)),
        name="gather_attn",
    )(q, kvn, *args)


def kernel(x_prompt, x_sample, c_prompt, c_sample, state_C, state_n, state_m, state_conv,
           cache_k_g0, cache_v_g0, cache_k_g1, cache_v_g1, cache_k_g2, cache_v_g2,
           w_ada, b_ada, g_norm, w_mlp_up, w_mlp_down, w_a_in, b_a_gate, w_a_conv, b_a_conv,
           w_a_out, g_kv, w_ada_kv, b_ada_kv, w_kv, w_b_q, w_b_o):
    bp, t, d = x_prompt.shape
    s_ = x_sample.shape[0]
    assert x_sample.shape[1] == 1 and w_ada.shape[0] == 2
    di = w_a_out.shape[1]
    dk = di // H_A
    dg = H_G * HD_B
    tn = 1024

    def chunks(w, n):
        return w.astype(BF16).reshape(w.shape[0], n // tn, tn).transpose(1, 0, 2)

    w_in = w_a_in[0]
    w_in4 = chunks(w_in[:, :4 * di], 4 * di)
    w_gate = jnp.pad(w_in[:, 4 * di:], ((0, 0), (0, LANES - 2 * H_A))).astype(BF16)
    b_gate = jnp.pad(b_a_gate[0], (0, LANES - 2 * H_A)).reshape(1, LANES)
    w_q3 = chunks(w_b_q[0], N_GROUPS * dg)
    w_kv6 = chunks(w_kv, 2 * N_GROUPS * dg)
    w_out = w_a_out[0].astype(BF16)
    w_o = w_b_o[0].astype(BF16)
    w_up = w_mlp_up.astype(BF16)
    w_dn = w_mlp_down.astype(BF16)
    cw = w_a_conv[0]
    cb = b_a_conv[0].reshape(1, 2 * di)
    g_kv2 = g_kv.reshape(1, d)

    c_all = jnp.concatenate([c_sample, c_prompt], axis=0)
    b_ada3 = b_ada.reshape(2, 1, 6 * d)
    mods = [_ada(c_all, w_ada, b_ada3, layer) for layer in range(2)]
    mod_kv = _ada(c_all, w_ada_kv.reshape(1, d, 2 * d), b_ada_kv.reshape(1, 1, 2 * d), 0)
    mp = [m[s_:].reshape(bp, 1, -1) for m in mods + [mod_kv]]
    ms = [m[:s_].reshape(1, s_, -1) for m in mods + [mod_kv]]

    conv0 = jnp.zeros((bp, SUBLANES, 2 * di), F32)
    qk, v, o_pre, gates, conv_new = _inproj_prompt(
        x_prompt, g_norm[0, 0:1], mp[0], w_in4, w_gate, cw, cb, conv0, tm=512)
    hg, p_c, p_n, p_m = _mlstm_prompt(
        qk, v, o_pre, gates, b_gate,
        jnp.zeros((bp, H_A, dk, dk), F32), jnp.zeros((bp, H_A, dk), F32),
        jnp.zeros((bp, 1, LANES), F32), L=256)
    x1 = _post(hg, x_prompt, mp[0], g_norm[0], w_out, w_up[0], w_dn[0], tm=512)
    kv_d = _proj_dil(x1, g_kv2, mp[2], 1, 0, w_kv6, tm=512)
    w_kv23 = w_kv6.reshape(2, N_GROUPS, d, tn)
    p_kv = []
    for gi, (win, _) in enumerate(GROUPS_B):
        kv_w = _proj(x1, g_kv2, mp[2], 1, 0, w_kv23[:, gi], tm=min(1024, win), last_rows=win)
        for half in range(2):
            p_kv.append(kv_w[:, :, half * dg:(half + 1) * dg].reshape(bp, win, H_G, HD_B))
    q_d = _proj_dil(x1, g_norm[1, 0:1], mp[1], 1, 0, w_q3, tm=512, out_scale=HD_B ** -0.5)
    outs, lses = [], []
    for gi in range(N_GROUPS):
        o_g, l_g = _band_attention(q_d[gi], kv_d[gi], kv_d[N_GROUPS + gi], gi)
        outs.append(o_g)
        lses.append(l_g)
    y_prompt = _post(outs, x1, mp[1], g_norm[1], w_o, w_up[1], w_dn[1], tm=512, lses=lses)

    xs = x_sample.reshape(1, s_, d)
    pj = _proj(xs, g_norm[0, 0:1], ms[0], 1, 0, w_in4, tm=s_)
    gts = _proj(xs, g_norm[0, 0:1], ms[0], 1, 0, w_gate.reshape(1, d, LANES), tm=s_)
    m0 = jnp.pad(state_m[0], ((0, 0), (0, LANES - H_A))).reshape(s_, 1, LANES)
    hgs, s_c, s_n, s_m, s_conv = _mlstm_step(
        pj.reshape(s_, 1, 4 * di), gts.reshape(s_, 1, LANES), b_gate, cw, cb,
        state_conv[0], state_C[0], state_n[0], m0)
    xs1 = _post(hgs.reshape(1, s_, di), xs, ms[0], g_norm[0], w_out, w_up[0], w_dn[0], tm=s_)
    kv_s = _proj(xs1, g_kv2, ms[2], 1, 0, w_kv6, tm=s_)
    q_s = _proj(xs1, g_norm[1, 0:1], ms[1], 1, 0, w_q3, tm=s_)
    caches = [(cache_k_g0, cache_v_g0), (cache_k_g1, cache_v_g1), (cache_k_g2, cache_v_g2)]
    att = _gather_attention(q_s.reshape(s_, N_GROUPS * H_G, HD_B),
                            kv_s.reshape(s_, 2 * N_GROUPS * H_G, HD_B), caches)
    y_sample = _post(att.reshape(1, s_, dg), xs1, ms[1], g_norm[1], w_o, w_up[1], w_dn[1], tm=s_)
    s_kv = []
    for gi in range(N_GROUPS):
        for half in range(2):
            col = (half * N_GROUPS + gi) * dg
            s_kv.append(kv_s[0, :, col:col + dg].reshape(s_, 1, H_G, HD_B))

    return (y_prompt, y_sample.reshape(s_, 1, d),
            p_c[None], p_n[None], p_m[:, 0, :H_A][None], conv_new[:, SUBLANES - (CONV_W - 1):][None],
            s_c[None], s_n[None], s_m[:, 0, :H_A][None], s_conv[None],
            *p_kv, *s_kv)
```

```python
import functools

import jax
import jax.numpy as jnp
from jax import lax
from jax.experimental import pallas as pl
from jax.experimental.pallas import tpu as pltpu

F32 = jnp.float32
BF16 = jnp.bfloat16
EPS = 1e-6
NEG_INF = float("-inf")

H_A = 4
CONV_W = 4
GROUPS_B = ((128, 1), (512, 4), (2048, 16))
N_GROUPS = len(GROUPS_B)
DILS = tuple(dil for _, dil in GROUPS_B)
H_G = 8
HD_B = 128
BAND = 128
LANES = 128
SUBLANES = 8
VMEM_LIMIT = 56 * 1024 * 1024


def _params(sem):
    return pltpu.CompilerParams(dimension_semantics=sem, vmem_limit_bytes=VMEM_LIMIT)


def _rms(x):
    return x * lax.rsqrt(jnp.mean(x * x, axis=-1, keepdims=True) + EPS)


def _sigmoid(x):
    return 0.5 * (jnp.tanh(0.5 * x) + 1.0)


def _log_sigmoid(x):
    return jnp.minimum(x, 0.0) - jnp.log1p(jnp.exp(-jnp.abs(x)))


def _resident(shape):
    nd = len(shape)
    return pl.BlockSpec(shape, lambda *_: (0,) * nd, pipeline_mode=pl.Buffered(1))


def _mod_spec(tm_mod, d, col):
    if tm_mod == 1:
        return pl.BlockSpec((1, 1, d), lambda b, i, *_: (b, 0, col))
    return pl.BlockSpec((1, tm_mod, d), lambda b, i, *_: (b, i, col))


def _ada_kernel(c_ref, w_ref, b_ref, o_ref):
    c = c_ref[...]
    s = (c * _sigmoid(c)).astype(BF16)
    o_ref[...] = jnp.dot(s, w_ref[...].astype(BF16), preferred_element_type=F32) + b_ref[...]


def _ada(c, w3, b3, layer):
    r, d = c.shape
    n = w3.shape[-1]
    tn = 1024
    return pl.pallas_call(
        _ada_kernel,
        grid=(n // tn,),
        in_specs=[pl.BlockSpec((r, d), lambda j: (0, 0)),
                  pl.BlockSpec((None, d, tn), lambda j: (layer, 0, j)),
                  pl.BlockSpec((None, 1, tn), lambda j: (layer, 0, j))],
        out_specs=pl.BlockSpec((r, tn), lambda j: (0, j)),
        out_shape=jax.ShapeDtypeStruct((r, n), F32),
        compiler_params=_params(("arbitrary",)),
        name="ada_mod",
    )(c, w3, b3)


def _proj_kernel(x_ref, g_ref, sc_ref, sh_ref, w_ref, o_ref, h_scr):
    j = pl.program_id(2)

    @pl.when(j == 0)
    def _():
        y = _rms(x_ref[0]) * g_ref[...]
        h_scr[...] = (y * (1.0 + sc_ref[0]) + sh_ref[0]).astype(BF16)

    o_ref[0] = jnp.dot(h_scr[...], w_ref[j], preferred_element_type=F32)


def _proj(x, g, mod, sc_col, sh_col, w3, *, tm, last_rows=None):
    b, t, d = x.shape
    nj, _, tn = w3.shape
    tm = min(tm, t)
    i_off = 0
    if last_rows is not None:
        assert mod.shape[1] == 1 and last_rows % tm == 0 and t % tm == 0
        i_off = (t - last_rows) // tm
        t = last_rows
    ni = t // tm
    tm_mod = 1 if mod.shape[1] == 1 else tm
    return pl.pallas_call(
        _proj_kernel,
        grid=(b, ni, nj),
        in_specs=[pl.BlockSpec((1, tm, d), lambda b_, i, j: (b_, i + i_off, 0)),
                  _resident((1, d)),
                  _mod_spec(tm_mod, d, sc_col),
                  _mod_spec(tm_mod, d, sh_col),
                  _resident((nj, d, tn))],
        out_specs=pl.BlockSpec((1, tm, tn), lambda b_, i, j: (b_, i, j)),
        out_shape=jax.ShapeDtypeStruct((b, t, nj * tn), F32),
        scratch_shapes=[pltpu.VMEM((tm, d), BF16)],
        compiler_params=_params(("arbitrary", "arbitrary", "arbitrary")),
        name="proj",
    )(x, g, mod, mod, w3)


def _proj_win_kernel(x_ref, g_ref, sc_ref, sh_ref, w_ref, k_ref, v_ref, h_scr, *, tm):
    j = pl.program_id(2)

    @pl.when(j == 0)
    def _():
        y = _rms(x_ref[0]) * g_ref[...]
        h_scr[...] = (y * (1.0 + sc_ref[0]) + sh_ref[0]).astype(BF16)

    acc = jnp.dot(h_scr[...], w_ref[j], preferred_element_type=F32)
    for jj, o_ref in enumerate((k_ref, v_ref)):
        @pl.when(j == jj)
        def _(o_ref=o_ref):
            for hh in range(H_G):
                o_ref[0, pl.ds(hh, tm, stride=H_G), :] = acc[:, hh * HD_B:(hh + 1) * HD_B]


def _proj_window(x, g, mod, sc_col, sh_col, w2, *, win):
    b, t, d = x.shape
    nj, _, tn = w2.shape
    assert nj == 2 and tn == H_G * HD_B
    tm = min(1024, win)
    i_off = (t - win) // tm
    out = jax.ShapeDtypeStruct((b, win * H_G, HD_B), F32)
    ospec = pl.BlockSpec((1, tm * H_G, HD_B), lambda b_, i, j: (b_, i, 0))
    k, v = pl.pallas_call(
        functools.partial(_proj_win_kernel, tm=tm),
        grid=(b, win // tm, nj),
        in_specs=[pl.BlockSpec((1, tm, d), lambda b_, i, j: (b_, i + i_off, 0)),
                  _resident((1, d)), _mod_spec(1, d, sc_col), _mod_spec(1, d, sh_col),
                  _resident((nj, d, tn))],
        out_specs=[ospec, ospec],
        out_shape=[out, out],
        scratch_shapes=[pltpu.VMEM((tm, d), BF16)],
        compiler_params=_params(("arbitrary", "arbitrary", "arbitrary")),
        name="proj_window",
    )(x, g, mod, mod, w2)
    return k.reshape(b, win, H_G, HD_B), v.reshape(b, win, H_G, HD_B)


def _proj_dil_kernel(x_ref, g_ref, sc_ref, sh_ref, w_ref, *rest, nj, out_scale, tm):
    outs = rest[:nj]
    h_scr, slab = rest[nj:]
    y = _rms(x_ref[0]) * g_ref[...]
    h = y * (1.0 + sc_ref[0]) + sh_ref[0]
    nslab = h.shape[1] // LANES
    for c in range(nslab):
        slab[c] = h[:, c * LANES:(c + 1) * LANES]
    for gi, dil in enumerate(DILS):
        if dil == 1:
            h_scr[gi] = h.astype(BF16)
            continue
        rows = tm // dil
        for r in range(dil):
            for c in range(nslab):
                h_scr[gi, r * rows:(r + 1) * rows, c * LANES:(c + 1) * LANES] = (
                    slab[c, pl.ds(r, rows, stride=dil), :].astype(BF16))
    for jj in range(nj):
        dil = DILS[jj % N_GROUPS]
        rows = tm // dil
        acc = jnp.dot(h_scr[jj % N_GROUPS], w_ref[jj], preferred_element_type=F32)
        for hh in range(H_G):
            for r in range(dil):
                blk = acc[r * rows:(r + 1) * rows, hh * HD_B:(hh + 1) * HD_B]
                if out_scale != 1.0:
                    blk = blk * out_scale
                outs[jj][0, hh, :, r * HD_B:(r + 1) * HD_B] = blk.astype(BF16)


def _proj_dil(x, g, mod, sc_col, sh_col, w3, *, tm, out_scale=1.0):
    b, t, d = x.shape
    nj, _, tn = w3.shape
    assert tn == H_G * HD_B and tm % (16 * max(DILS)) == 0 and t % tm == 0
    ni = t // tm
    in_specs = [pl.BlockSpec((1, tm, d), lambda b_, i: (b_, i, 0)),
                _resident((1, d)), _mod_spec(1, d, sc_col), _mod_spec(1, d, sh_col),
                _resident((nj, d, tn))]
    out_shape, out_specs = [], []
    for jj in range(nj):
        dil = DILS[jj % N_GROUPS]
        out_shape.append(jax.ShapeDtypeStruct((b, H_G, t // dil, dil * HD_B), BF16))
        out_specs.append(pl.BlockSpec((1, H_G, tm // dil, dil * HD_B), lambda b_, i: (b_, 0, i, 0)))
    kern = functools.partial(_proj_dil_kernel, nj=nj, out_scale=out_scale, tm=tm)
    return pl.pallas_call(
        kern,
        grid=(b, ni),
        in_specs=in_specs,
        out_specs=out_specs,
        out_shape=out_shape,
        scratch_shapes=[pltpu.VMEM((N_GROUPS, tm, d), BF16),
                        pltpu.VMEM((d // LANES, tm, LANES), F32)],
        compiler_params=_params(("arbitrary", "arbitrary")),
        name="proj_dil",
    )(x, g, mod, mod, w3)


def _inproj_kernel(x_ref, g_ref, sc_ref, sh_ref, w_ref, wg_ref, cw_ref, cb_ref, c0_ref,
                   qk_ref, v_ref, o_ref, gt_ref, cn_ref, ext_scr, *, tm, k_scale):
    i = pl.program_id(1)
    y = _rms(x_ref[0]) * g_ref[...]
    h = (y * (1.0 + sc_ref[0]) + sh_ref[0]).astype(BF16)

    @pl.when(i == 0)
    def _():
        ext_scr[0:SUBLANES, :] = c0_ref[0]

    di = w_ref.shape[2]
    for c in range(2):
        cs = slice(c * di, (c + 1) * di)
        ext_scr[SUBLANES:, cs] = jnp.dot(h, w_ref[c], preferred_element_type=F32)
        yv = cb_ref[:, cs]
        for k in range(CONV_W):
            yv = yv + cw_ref[CONV_W - 1 - k:CONV_W - k, cs] * ext_scr[SUBLANES - k:SUBLANES - k + tm, cs]
        yv = yv * _sigmoid(yv)
        if c == 1:
            yv = yv * k_scale
        qk_ref[0, :, cs] = yv.astype(BF16)
    v_ref[0] = jnp.dot(h, w_ref[2], preferred_element_type=F32).astype(BF16)
    o_ref[0] = jnp.dot(h, w_ref[3], preferred_element_type=F32)
    gt_ref[0] = jnp.dot(h, wg_ref[...], preferred_element_type=F32)
    last = ext_scr[tm:tm + SUBLANES, :]
    cn_ref[0] = last
    ext_scr[0:SUBLANES, :] = last


def _inproj_prompt(x, g, mod, w4, wg, cw, cb, conv0, *, tm):
    b, t, d = x.shape
    di = w4.shape[2]
    ni = t // tm
    dk = di // H_A
    kern = functools.partial(_inproj_kernel, tm=tm, k_scale=dk ** -0.5)
    row = lambda n: pl.BlockSpec((1, tm, n), lambda b_, i: (b_, i, 0))
    return pl.pallas_call(
        kern,
        grid=(b, ni),
        in_specs=[row(d), _resident((1, d)), _mod_spec(1, d, 1), _mod_spec(1, d, 0),
                  _resident(w4.shape), _resident(wg.shape), _resident(cw.shape), _resident(cb.shape),
                  pl.BlockSpec((1, SUBLANES, 2 * di), lambda b_, i: (b_, 0, 0))],
        out_specs=[row(2 * di), row(di), row(di), row(LANES),
                   pl.BlockSpec((1, SUBLANES, 2 * di), lambda b_, i: (b_, 0, 0))],
        out_shape=[jax.ShapeDtypeStruct((b, t, 2 * di), BF16),
                   jax.ShapeDtypeStruct((b, t, di), BF16),
                   jax.ShapeDtypeStruct((b, t, di), F32),
                   jax.ShapeDtypeStruct((b, t, LANES), F32),
                   jax.ShapeDtypeStruct((b, SUBLANES, 2 * di), F32)],
        scratch_shapes=[pltpu.VMEM((tm + SUBLANES, 2 * di), F32)],
        compiler_params=_params(("arbitrary", "arbitrary")),
        name="inproj_prompt",
    )(x, g, mod, mod, w4, wg, cw, cb, conv0)


def _mlstm_kernel(qk_ref, v_ref, o_ref, gt_ref, bg_ref, c0_ref, n0_ref, m0_ref,
                  hg_ref, c_ref, n_ref, m_ref, *, L, dk):
    i = pl.program_id(1)

    @pl.when(i == 0)
    def _():
        c_ref[...] = c0_ref[...]
        n_ref[...] = n0_ref[...]
        m_ref[...] = m0_ref[...]

    di = H_A * dk
    g = gt_ref[0] + bg_ref[...]
    lane = lax.broadcasted_iota(jnp.int32, (L, LANES), 1)
    gl = jnp.where(lane < H_A, g, _log_sigmoid(g))
    gl_t = gl.T
    r8 = gl_t[0:SUBLANES, :]
    pos = lax.broadcasted_iota(jnp.int32, (SUBLANES, L), 1)
    cs = r8
    s = 1
    while s < L:
        cs = cs + jnp.where(pos >= s, pltpu.roll(cs, s, axis=1), 0.0)
        s *= 2
    cs_t = jnp.concatenate([cs, jnp.zeros((LANES - SUBLANES, L), F32)], axis=0).T
    row = lax.broadcasted_iota(jnp.int32, (L, L), 0)
    col = lax.broadcasted_iota(jnp.int32, (L, L), 1)
    causal = col <= row
    m_all = m_ref[0]
    lane1 = lax.broadcasted_iota(jnp.int32, (1, LANES), 1)
    for h in range(H_A):
        q = qk_ref[0, :, h * dk:(h + 1) * dk]
        k = qk_ref[0, :, di + h * dk:di + (h + 1) * dk]
        v = v_ref[0, :, h * dk:(h + 1) * dk]
        b_col = cs_t[:, H_A + h:H_A + h + 1]
        b_row = cs[H_A + h:H_A + h + 1, :]
        li_row = r8[h:h + 1, :]
        li_col = gl[:, h:h + 1]
        m_prev = m_all[:, h:h + 1]
        dmat = jnp.where(causal, b_col - b_row + li_row, NEG_INF)
        inter = b_col + m_prev
        m_t = jnp.maximum(inter, jnp.max(dmat, axis=1, keepdims=True))
        w_inter = jnp.exp(inter - m_t)
        w_intra = jnp.exp(dmat - m_t)
        s_mat = lax.dot_general(q, k, (((1,), (1,)), ((), ())), preferred_element_type=F32) * w_intra
        cm = c_ref[0, h]
        nv = n_ref[0, h:h + 1, :]
        num = (w_inter * jnp.dot(q, cm.astype(BF16), preferred_element_type=F32)
               + jnp.dot(s_mat.astype(BF16), v, preferred_element_type=F32))
        qn = jnp.sum(q.astype(F32) * nv, axis=1, keepdims=True)
        den = w_inter * qn + jnp.sum(s_mat, axis=1, keepdims=True)
        hout = num / jnp.maximum(jnp.abs(den), jnp.exp(-m_t))
        hg_ref[0, :, h * dk:(h + 1) * dk] = (_sigmoid(o_ref[0, :, h * dk:(h + 1) * dk]) * hout).astype(BF16)
        m_last = m_t[L - 1:L, :]
        b_last = b_col[L - 1:L, :]
        wl_col = jnp.exp(b_last - b_col + li_col - m_last)
        w_last = w_inter[L - 1:L, :]
        kw = k.astype(F32) * wl_col
        c_ref[0, h] = w_last * cm + lax.dot_general(
            kw.astype(BF16), v, (((0,), (0,)), ((), ())), preferred_element_type=F32)
        n_ref[0, h:h + 1, :] = w_last * nv + jnp.sum(kw, axis=0, keepdims=True)
        m_all = jnp.where(lane1 == h, m_last, m_all)
    m_ref[0] = m_all


def _mlstm_prompt(qk, v, o_pre, gates, bg, c0, n0, m0, *, L):
    b, t, di = v.shape
    dk = di // H_A
    nc = t // L
    kern = functools.partial(_mlstm_kernel, L=L, dk=dk)
    row = lambda n: pl.BlockSpec((1, L, n), lambda b_, i: (b_, i, 0))
    st4 = pl.BlockSpec((1, H_A, dk, dk), lambda b_, i: (b_, 0, 0, 0))
    st3 = pl.BlockSpec((1, H_A, dk), lambda b_, i: (b_, 0, 0))
    stm = pl.BlockSpec((1, 1, LANES), lambda b_, i: (b_, 0, 0))
    return pl.pallas_call(
        kern,
        grid=(b, nc),
        in_specs=[row(2 * di), row(di), row(di), row(LANES), _resident((1, LANES)), st4, st3, stm],
        out_specs=[row(di), st4, st3, stm],
        out_shape=[jax.ShapeDtypeStruct((b, t, di), BF16),
                   jax.ShapeDtypeStruct((b, H_A, dk, dk), F32),
                   jax.ShapeDtypeStruct((b, H_A, dk), F32),
                   jax.ShapeDtypeStruct((b, 1, LANES), F32)],
        compiler_params=_params(("arbitrary", "arbitrary")),
        name="mlstm_prompt",
    )(qk, v, o_pre, gates, bg, c0, n0, m0)


def _mlstm_step_kernel(pj_ref, gt_ref, bg_ref, cw_ref, cb_ref, cv_ref, c0_ref, n0_ref, m0_ref,
                       hg_ref, c_ref, n_ref, m_ref, cvn_ref, *, dk):
    di = H_A * dk
    pj = pj_ref[0]
    u = pj[:, 0:2 * di]
    buf = cv_ref[0]
    yv = cb_ref[...] + cw_ref[CONV_W - 1:CONV_W, :] * u
    for j in range(CONV_W - 1):
        yv = yv + cw_ref[j:j + 1, :] * buf[j:j + 1, :]
    yv = yv * _sigmoid(yv)
    cvn_ref[0] = jnp.concatenate([buf[1:CONV_W - 1, :], u], axis=0)
    g = gt_ref[0] + bg_ref[...]
    m_all = m0_ref[0]
    lane1 = lax.broadcasted_iota(jnp.int32, (1, LANES), 1)
    row8 = lax.broadcasted_iota(jnp.int32, (SUBLANES, dk), 0)
    for h in range(H_A):
        q = yv[:, h * dk:(h + 1) * dk]
        k = yv[:, di + h * dk:di + (h + 1) * dk] * (dk ** -0.5)
        v = pj[:, 2 * di + h * dk:2 * di + (h + 1) * dk]
        o_pre = pj[:, 3 * di + h * dk:3 * di + (h + 1) * dk]
        li = g[:, h:h + 1]
        lf = _log_sigmoid(g[:, H_A + h:H_A + h + 1])
        m_prev = m_all[:, h:h + 1]
        inter = lf + m_prev
        m_t = jnp.maximum(inter, li)
        w_inter = jnp.exp(inter - m_t)
        w_intra = jnp.exp(li - m_t)
        cm = c0_ref[0, h]
        nv = n0_ref[0, h:h + 1, :]
        q8 = jnp.where(row8 == 0, jnp.broadcast_to(q, (SUBLANES, dk)), 0.0).astype(BF16)
        s = jnp.sum(q * k, axis=1, keepdims=True) * w_intra
        qc = jnp.dot(q8, cm.astype(BF16), preferred_element_type=F32)[0:1, :]
        num = w_inter * qc + s * v
        den = w_inter * jnp.sum(q * nv, axis=1, keepdims=True) + s
        hout = num / jnp.maximum(jnp.abs(den), jnp.exp(-m_t))
        hg_ref[0, :, h * dk:(h + 1) * dk] = _sigmoid(o_pre) * hout
        kw = k * w_intra
        kw8 = jnp.where(row8 == 0, jnp.broadcast_to(kw, (SUBLANES, dk)), 0.0).astype(BF16)
        v8 = jnp.where(row8 == 0, jnp.broadcast_to(v, (SUBLANES, dk)), 0.0).astype(BF16)
        c_ref[0, h] = w_inter * cm + lax.dot_general(
            kw8, v8, (((0,), (0,)), ((), ())), preferred_element_type=F32)
        n_ref[0, h:h + 1, :] = w_inter * nv + kw
        m_all = jnp.where(lane1 == h, m_t, m_all)
    m_ref[0] = m_all


def _mlstm_step(pj, gates, bg, cw, cb, conv, c0, n0, m0):
    s_, _, n4 = pj.shape
    di = n4 // 4
    dk = di // H_A
    kern = functools.partial(_mlstm_step_kernel, dk=dk)
    per = lambda *shape: pl.BlockSpec((1,) + shape, lambda b_: (b_,) + (0,) * len(shape))
    return pl.pallas_call(
        kern,
        grid=(s_,),
        in_specs=[per(1, n4), per(1, LANES), _resident((1, LANES)), _resident(cw.shape),
                  _resident(cb.shape), per(CONV_W - 1, 2 * di), per(H_A, dk, dk), per(H_A, dk),
                  per(1, LANES)],
        out_specs=[per(1, di), per(H_A, dk, dk), per(H_A, dk), per(1, LANES), per(CONV_W - 1, 2 * di)],
        out_shape=[jax.ShapeDtypeStruct((s_, 1, di), F32),
                   jax.ShapeDtypeStruct((s_, H_A, dk, dk), F32),
                   jax.ShapeDtypeStruct((s_, H_A, dk), F32),
                   jax.ShapeDtypeStruct((s_, 1, LANES), F32),
                   jax.ShapeDtypeStruct((s_, CONV_W - 1, 2 * di), F32)],
        compiler_params=_params(("arbitrary",)),
        name="mlstm_step",
    )(pj, gates, bg, cw, cb, conv, c0, n0, m0)


def _post_kernel(*refs, attn, ff_chunk, tm):
    if attn:
        og = refs[0:N_GROUPS]
        ls = refs[N_GROUPS:2 * N_GROUPS]
        rest = refs[2 * N_GROUPS:-2]
        nat, natl = refs[-2:]
        nhb = natl.shape[1]
        hps = H_G // nhb
        for gi, dil in enumerate(DILS):
            rows = tm // dil
            for r in range(dil):
                dst = slice(None) if dil == 1 else pl.ds(r, rows, stride=dil)
                for h in range(H_G):
                    nat[gi, h, dst, :] = og[gi][0, h, :, r * HD_B:(r + 1) * HD_B].astype(F32)
                for hb in range(nhb):
                    natl[gi, hb, dst, :] = ls[gi][0, hb, :, r * LANES:(r + 1) * LANES]
        w = []
        for hb in range(nhb):
            l = [natl[gi, hb] for gi in range(N_GROUPS)]
            mx = functools.reduce(jnp.maximum, l)
            e = [jnp.exp(a - mx) for a in l]
            inv = 1.0 / functools.reduce(lambda a, b_: a + b_, e)
            w.append([a * inv for a in e])
        parts = []
        for h in range(H_G):
            hb, hl = divmod(h, hps)
            acc_h = w[hb][0][:, hl:hl + 1] * nat[0, h]
            for gi in range(1, N_GROUPS):
                acc_h = acc_h + w[hb][gi][:, hl:hl + 1] * nat[gi, h]
            parts.append(acc_h.astype(BF16))
        lhs = jnp.concatenate(parts, axis=1)
    else:
        lhs = refs[0][0].astype(BF16)
        rest = refs[1:]
    x_ref, gt1_ref, sh2_ref, sc2_ref, gt2_ref, gn_ref, wo_ref, wup_ref, wdn_ref, out_ref = rest
    a = jnp.dot(lhs, wo_ref[...], preferred_element_type=F32)
    x1 = x_ref[0] + gt1_ref[0] * (_rms(a) * gn_ref[1:2, :])
    h2 = (_rms(x1) * gn_ref[2:3, :] * (1.0 + sc2_ref[0]) + sh2_ref[0]).astype(BF16)
    dff = wup_ref.shape[1]
    acc = None
    for c in range(dff // ff_chunk):
        cs = slice(c * ff_chunk, (c + 1) * ff_chunk)
        u = jnp.maximum(jnp.dot(h2, wup_ref[:, cs], preferred_element_type=F32), 0.0)
        p = jnp.dot((u * u).astype(BF16), wdn_ref[cs, :], preferred_element_type=F32)
        acc = p if acc is None else acc + p
    out_ref[0] = x1 + gt2_ref[0] * (_rms(acc) * gn_ref[3:4, :])


def _post(lhs, x, mod, gn, wo, wup, wdn, *, tm, lses=None):
    b, t, d = x.shape
    tm = min(tm, t)
    ni = t // tm
    tm_mod = 1 if mod.shape[1] == 1 else tm
    attn = lses is not None
    row = lambda n: pl.BlockSpec((1, tm, n), lambda b_, i: (b_, i, 0))
    scratch = []
    if attn:
        nhb = lses[0].shape[1]
        lhs_specs = ([pl.BlockSpec((1, H_G, tm // dil, dil * HD_B), lambda b_, i: (b_, 0, i, 0))
                      for dil in DILS]
                     + [pl.BlockSpec((1, nhb, tm // dil, dil * LANES), lambda b_, i: (b_, 0, i, 0))
                        for dil in DILS])
        lhs_args = list(lhs) + list(lses)
        scratch = [pltpu.VMEM((N_GROUPS, H_G, tm, HD_B), F32),
                   pltpu.VMEM((N_GROUPS, nhb, tm, LANES), F32)]
    else:
        lhs_specs = [row(lhs.shape[-1])]
        lhs_args = [lhs]
    kern = functools.partial(_post_kernel, attn=attn, ff_chunk=1024, tm=tm)
    return pl.pallas_call(
        kern,
        grid=(b, ni),
        in_specs=lhs_specs + [row(d), _mod_spec(tm_mod, d, 2), _mod_spec(tm_mod, d, 3),
                              _mod_spec(tm_mod, d, 4), _mod_spec(tm_mod, d, 5),
                              _resident(gn.shape), _resident(wo.shape), _resident(wup.shape),
                              _resident(wdn.shape)],
        out_specs=row(d),
        out_shape=jax.ShapeDtypeStruct((b, t, d), F32),
        scratch_shapes=scratch,
        compiler_params=_params(("arbitrary", "arbitrary")),
        name="post_mlp",
    )(*lhs_args, x, mod, mod, mod, mod, gn, wo, wup, wdn)


def _band_kernel(q_ref, k_ref, v_ref, kh_ref, vh_ref, o_ref, l_ref, *, dil, nb, hps):
    n = pl.program_id(2)
    iq = lax.broadcasted_iota(jnp.int32, (BAND, 2 * BAND), 0)
    ik = lax.broadcasted_iota(jnp.int32, (BAND, 2 * BAND), 1)
    band = (ik >= iq) & (ik <= iq + BAND)
    lane = lax.broadcasted_iota(jnp.int32, (BAND, LANES), 1)

    def block(q, kw, vw, ok, r):
        s = jnp.einsum("hqd,hkd->hqk", q, kw, preferred_element_type=F32)
        s = jnp.where(ok[None], s, NEG_INF)
        m = jnp.max(s, axis=-1, keepdims=True)
        p = jnp.exp(s - m)
        den = jnp.sum(p, axis=-1, keepdims=True)
        o = jnp.einsum("hqk,hkd->hqd", p.astype(BF16), vw, preferred_element_type=F32) / den
        lse = m + jnp.log(den)
        tile = jnp.zeros((BAND, LANES), F32)
        for h in range(hps):
            tile = jnp.where(lane == h, lse[h], tile)
        return o, tile

    for r in range(dil):
        ls = slice(r * HD_B, (r + 1) * HD_B)
        kw0 = jnp.concatenate([kh_ref[0, :, :, ls], k_ref[0, :, 0:BAND, ls]], axis=1)
        vw0 = jnp.concatenate([vh_ref[0, :, :, ls], v_ref[0, :, 0:BAND, ls]], axis=1)
        kmin = jnp.where(n == 0, BAND, 0)
        o, tile = block(q_ref[0, :, 0:BAND, ls], kw0, vw0, band & (ik >= kmin), r)
        o_ref[0, :, 0:BAND, ls] = o.astype(o_ref.dtype)
        l_ref[0, 0, 0:BAND, ls] = tile

        def body(jb, carry, r=r, ls=ls):
            r0 = pl.multiple_of(jb * BAND, BAND)
            rk = pl.multiple_of((jb - 1) * BAND, BAND)
            o, tile = block(q_ref[0, :, pl.ds(r0, BAND), ls], k_ref[0, :, pl.ds(rk, 2 * BAND), ls],
                            v_ref[0, :, pl.ds(rk, 2 * BAND), ls], band, r)
            o_ref[0, :, pl.ds(r0, BAND), ls] = o.astype(o_ref.dtype)
            l_ref[0, 0, pl.ds(r0, BAND), ls] = tile
            return carry

        if nb > 1:
            lax.fori_loop(1, nb, body, 0, unroll=3 if (nb - 1) % 3 == 0 else 1)


def _band_attention(q, k, v, grp, *, tokens_per_step=2048, hps=4):
    b, _, tsub, w = q.shape
    dil = DILS[grp]
    rows = tokens_per_step // dil
    nb = rows // BAND
    nsteps = tsub // rows
    nhb = H_G // hps
    cur = pl.BlockSpec((1, hps, rows, w), lambda b_, hb, n: (b_, hb, n, 0))
    halo = pl.BlockSpec((1, hps, BAND, w), lambda b_, hb, n: (b_, hb, jnp.maximum(n * nb - 1, 0), 0))
    kern = functools.partial(_band_kernel, dil=dil, nb=nb, hps=hps)
    return pl.pallas_call(
        kern,
        grid=(b, nhb, nsteps),
        in_specs=[cur, cur, cur, halo, halo],
        out_specs=[cur, pl.BlockSpec((1, 1, rows, w), lambda b_, hb, n: (b_, hb, n, 0))],
        out_shape=[jax.ShapeDtypeStruct((b, H_G, tsub, w), BF16),
                   jax.ShapeDtypeStruct((b, nhb, tsub, w), F32)],
        compiler_params=_params(("arbitrary", "arbitrary", "arbitrary")),
        name="band_attn_g%d" % grp,
    )(q, k, v, k, v)


def _gather_attn_kernel(q_ref, kv_ref, *rest, scale):
    caches = rest[0:2 * N_GROUPS]
    o_ref = rest[2 * N_GROUPS]
    outs, lses = [], []
    for gi in range(N_GROUPS):
        kc = caches[2 * gi][0]
        vc = caches[2 * gi + 1][0]
        qg = q_ref[0, gi * H_G:(gi + 1) * H_G, :]
        kn = kv_ref[0, gi * H_G:(gi + 1) * H_G, :]
        vn = kv_ref[0, (N_GROUPS + gi) * H_G:(N_GROUPS + gi + 1) * H_G, :]
        sc = jnp.sum(kc * qg[None], axis=-1, keepdims=True) * scale
        sn = jnp.sum(kn * qg, axis=-1, keepdims=True) * scale
        m = jnp.maximum(jnp.max(sc, axis=0), sn)
        pc = jnp.exp(sc - m[None])
        pn = jnp.exp(sn - m)
        den = jnp.sum(pc, axis=0) + pn
        outs.append((jnp.sum(pc * vc, axis=0) + pn * vn) / den)
        lses.append(m + jnp.log(den))
    mx = functools.reduce(jnp.maximum, lses)
    e = [jnp.exp(a - mx) for a in lses]
    inv = 1.0 / functools.reduce(lambda a, b_: a + b_, e)
    acc = (e[0] * inv) * outs[0]
    for gi in range(1, N_GROUPS):
        acc = acc + (e[gi] * inv) * outs[gi]
    o_ref[0] = acc


def _gather_attention(q, kvn, caches):
    s_ = q.shape[0]
    args, specs = [], []
    for gi, (win, dil) in enumerate(GROUPS_B):
        for c in caches[gi]:
            wb = c.shape[1]
            assert wb == win and wb // dil == BAND
            args.append(c.reshape(s_, BAND, dil, H_G, HD_B))
            specs.append(pl.BlockSpec((1, BAND, None, H_G, HD_B), lambda b_: (b_, 0, 0, 0, 0)))
    per = lambda n: pl.BlockSpec((1, n, HD_B), lambda b_: (b_, 0, 0))
    kern = functools.partial(_gather_attn_kernel, scale=HD_B ** -0.5)
    return pl.pallas_call(
        kern,
        grid=(s_,),
        in_specs=[per(N_GROUPS * H_G), per(2 * N_GROUPS * H_G)] + specs,
        out_specs=per(H_G),
        out_shape=jax.ShapeDtypeStruct((s_, H_G, HD_B), F32),
        compiler_params=_params(("arbitrary",)),
        name="gather_attn",
    )(q, kvn, *args)


def kernel(x_prompt, x_sample, c_prompt, c_sample, state_C, state_n, state_m, state_conv,
           cache_k_g0, cache_v_g0, cache_k_g1, cache_v_g1, cache_k_g2, cache_v_g2,
           w_ada, b_ada, g_norm, w_mlp_up, w_mlp_down, w_a_in, b_a_gate, w_a_conv, b_a_conv,
           w_a_out, g_kv, w_ada_kv, b_ada_kv, w_kv, w_b_q, w_b_o):
    bp, t, d = x_prompt.shape
    s_ = x_sample.shape[0]
    assert x_sample.shape[1] == 1 and w_ada.shape[0] == 2
    di = w_a_out.shape[1]
    dk = di // H_A
    dg = H_G * HD_B
    tn = 1024

    def chunks(w, n):
        return w.astype(BF16).reshape(w.shape[0], n // tn, tn).transpose(1, 0, 2)

    w_in = w_a_in[0]
    w_in4 = chunks(w_in[:, :4 * di], 4 * di)
    w_gate = jnp.pad(w_in[:, 4 * di:], ((0, 0), (0, LANES - 2 * H_A))).astype(BF16)
    b_gate = jnp.pad(b_a_gate[0], (0, LANES - 2 * H_A)).reshape(1, LANES)
    w_q3 = chunks(w_b_q[0], N_GROUPS * dg)
    w_kv6 = chunks(w_kv, 2 * N_GROUPS * dg)
    w_out = w_a_out[0].astype(BF16)
    w_o = w_b_o[0].astype(BF16)
    w_up = w_mlp_up.astype(BF16)
    w_dn = w_mlp_down.astype(BF16)
    cw = w_a_conv[0]
    cb = b_a_conv[0].reshape(1, 2 * di)
    g_kv2 = g_kv.reshape(1, d)

    c_all = jnp.concatenate([c_sample, c_prompt], axis=0)
    b_ada3 = b_ada.reshape(2, 1, 6 * d)
    mods = [_ada(c_all, w_ada, b_ada3, layer) for layer in range(2)]
    mod_kv = _ada(c_all, w_ada_kv.reshape(1, d, 2 * d), b_ada_kv.reshape(1, 1, 2 * d), 0)
    mp = [m[s_:].reshape(bp, 1, -1) for m in mods + [mod_kv]]
    ms = [m[:s_].reshape(1, s_, -1) for m in mods + [mod_kv]]

    conv0 = jnp.zeros((bp, SUBLANES, 2 * di), F32)
    qk, v, o_pre, gates, conv_new = _inproj_prompt(
        x_prompt, g_norm[0, 0:1], mp[0], w_in4, w_gate, cw, cb, conv0, tm=512)
    hg, p_c, p_n, p_m = _mlstm_prompt(
        qk, v, o_pre, gates, b_gate,
        jnp.zeros((bp, H_A, dk, dk), F32), jnp.zeros((bp, H_A, dk), F32),
        jnp.zeros((bp, 1, LANES), F32), L=256)
    x1 = _post(hg, x_prompt, mp[0], g_norm[0], w_out, w_up[0], w_dn[0], tm=512)
    kv_d = _proj_dil(x1, g_kv2, mp[2], 1, 0, w_kv6, tm=512)
    w_kv23 = w_kv6.reshape(2, N_GROUPS, d, tn)
    p_kv = []
    for gi, (win, _) in enumerate(GROUPS_B):
        p_kv.extend(_proj_window(x1, g_kv2, mp[2], 1, 0, w_kv23[:, gi], win=win))
    q_d = _proj_dil(x1, g_norm[1, 0:1], mp[1], 1, 0, w_q3, tm=512, out_scale=HD_B ** -0.5)
    outs, lses = [], []
    for gi in range(N_GROUPS):
        o_g, l_g = _band_attention(q_d[gi], kv_d[gi], kv_d[N_GROUPS + gi], gi)
        outs.append(o_g)
        lses.append(l_g)
    y_prompt = _post(outs, x1, mp[1], g_norm[1], w_o, w_up[1], w_dn[1], tm=512, lses=lses)

    xs = x_sample.reshape(1, s_, d)
    pj = _proj(xs, g_norm[0, 0:1], ms[0], 1, 0, w_in4, tm=s_)
    gts = _proj(xs, g_norm[0, 0:1], ms[0], 1, 0, w_gate.reshape(1, d, LANES), tm=s_)
    m0 = jnp.pad(state_m[0], ((0, 0), (0, LANES - H_A))).reshape(s_, 1, LANES)
    hgs, s_c, s_n, s_m, s_conv = _mlstm_step(
        pj.reshape(s_, 1, 4 * di), gts.reshape(s_, 1, LANES), b_gate, cw, cb,
        state_conv[0], state_C[0], state_n[0], m0)
    xs1 = _post(hgs.reshape(1, s_, di), xs, ms[0], g_norm[0], w_out, w_up[0], w_dn[0], tm=s_)
    kv_s = _proj(xs1, g_kv2, ms[2], 1, 0, w_kv6, tm=s_)
    q_s = _proj(xs1, g_norm[1, 0:1], ms[1], 1, 0, w_q3, tm=s_)
    caches = [(cache_k_g0, cache_v_g0), (cache_k_g1, cache_v_g1), (cache_k_g2, cache_v_g2)]
    att = _gather_attention(q_s.reshape(s_, N_GROUPS * H_G, HD_B),
                            kv_s.reshape(s_, 2 * N_GROUPS * H_G, HD_B), caches)
    y_sample = _post(att.reshape(1, s_, dg), xs1, ms[1], g_norm[1], w_o, w_up[1], w_dn[1], tm=s_)
    s_kv = []
    for gi in range(N_GROUPS):
        for half in range(2):
            col = (half * N_GROUPS + gi) * dg
            s_kv.append(kv_s[0, :, col:col + dg].reshape(s_, 1, H_G, HD_B))

    return (y_prompt, y_sample.reshape(s_, 1, d),
            p_c[None], p_n[None], p_m[:, 0, :H_A][None], conv_new[:, SUBLANES - (CONV_W - 1):][None],
            s_c[None], s_n[None], s_m[:, 0, :H_A][None], s_conv[None],
            *p_kv, *s_kv)
```
